```python
import math
import jax, jax.numpy as jnp
from jax import lax
import numpy as np

D_MODEL = 2048
BATCH = 1
SEQ = 8192
DEPTH = 2
DEC_BATCH = 128
DEC_SEQ = 8
PAST_LEN = 16384
PAGE_SIZE = 128

N_EVEN = (DEPTH + 1) // 2
N_ODD = DEPTH // 2
Q_BLOCK = 128
H_A = 8
KV_A = 2
G_A = H_A // KV_A
DQK_A = 64
DV_A = 2 * DQK_A
H_B = 8
KV_B = 2
G_B = H_B // KV_B
D_B = 128
FORGET_BIAS_INIT = 3.0
EVEN_SPLITS = (H_A * 2 * DQK_A, KV_A * 2 * DQK_A, KV_A * DV_A, H_B * D_B, KV_B * D_B, KV_B * D_B, H_B)
IN_EVEN = H_A * 2 * DQK_A + KV_A * 2 * DQK_A + KV_A * DV_A + H_B * D_B + 2 * KV_B * D_B + H_B
W_EVEN = H_A * DV_A + H_B * D_B
H_C = 16
NOPE_C = 128
ROPE_C = 64
V_C = 128
Q_LORA = 512
KV_LORA = 256
ROPE_THETA = 10000.0
N_GROUPS = 4
EXP_PER_GROUP = 8
N_EXPERTS = N_GROUPS * EXP_PER_GROUP
TOP_K_EXPERT = 2
D_EXPERT = 512
ALPHA = (2 * DEPTH) ** 0.25
BETA = (8 * DEPTH) ** -0.25
DIFF_SCALE = DQK_A ** -0.5
FOX_SCALE = D_B ** -0.5
MLA_SCALE = (NOPE_C + ROPE_C) ** -0.5
LN_EPS = 1e-5
RMS_EPS = 1e-6

kernel_name = 'hybrid_diffattn_fox_mla_hmoe_decode_step'


def layer_norm(x, g, b):
    xf = x.astype(jnp.float32)
    mu = jnp.mean(xf, axis=-1, keepdims=True)
    var = jnp.mean(jnp.square(xf - mu), axis=-1, keepdims=True)
    return ((xf - mu) * lax.rsqrt(var + LN_EPS) * g + b).astype(x.dtype)


def rms_norm(x, g, eps):
    xf = x.astype(jnp.float32)
    return (xf * lax.rsqrt(jnp.mean(jnp.square(xf), axis=-1, keepdims=True) + eps) * g).astype(x.dtype)


def rope(x, pos):
    half = x.shape[-1] // 2
    freqs = ROPE_THETA ** (-jnp.arange(half, dtype=jnp.float32) / half)
    ang = pos.astype(jnp.float32)[:, None] * freqs
    shape = (1, ang.shape[0]) + (1,) * (x.ndim - 3) + (half,)
    cos = jnp.cos(ang).reshape(shape)
    sin = jnp.sin(ang).reshape(shape)
    xf = x.astype(jnp.float32)
    x1, x2 = xf[..., :half], xf[..., half:]
    return jnp.concatenate([x1 * cos - x2 * sin, x1 * sin + x2 * cos], axis=-1).astype(x.dtype)


def alibi_slopes():
    return (2.0 ** (-8.0 * jnp.arange(1, H_A + 1, dtype=jnp.float32) / H_A)).reshape(KV_A, G_A)


def lambda_init(layer_idx):
    return 0.8 - 0.6 * math.exp(-0.3 * layer_idx)


def diff_lambda(lq1, lk1, lq2, lk2, lam_init):
    f = lambda a: a.astype(jnp.float32)
    return jnp.exp(jnp.sum(f(lq1) * f(lk1))) - jnp.exp(jnp.sum(f(lq2) * f(lk2))) + lam_init


def sweep_query_blocks(block_fn, seq_len):
    starts = jnp.arange(seq_len // Q_BLOCK, dtype=jnp.int32) * Q_BLOCK
    outs = lax.map(block_fn, starts)
    return jax.tree_util.tree_map(
        lambda o: jnp.moveaxis(o, 0, 1).reshape((o.shape[1], seq_len) + o.shape[3:]), outs)


def gather_pages(cache, li, pages):
    g = cache[li, pages]
    return g.reshape((-1,) + g.shape[2:])


def even_project(x, w_in, b_f):
    B, T, _ = x.shape
    h = jnp.einsum('btd,de->bte', x, w_in)
    idx = np.cumsum(EVEN_SPLITS)[:-1].tolist()
    qd, kd, vd, qf, kf, vf, fl = jnp.split(h, idx, axis=-1)
    qd = qd.reshape(B, T, KV_A, G_A, 2, DQK_A)
    kd = kd.reshape(B, T, KV_A, 2, DQK_A)
    vd = vd.reshape(B, T, KV_A, DV_A)
    qf = qf.reshape(B, T, KV_B, G_B, D_B)
    kf = kf.reshape(B, T, KV_B, D_B)
    vf = vf.reshape(B, T, KV_B, D_B)
    logf = jax.nn.log_sigmoid(fl.astype(jnp.float32) + b_f.astype(jnp.float32))
    return qd, kd, vd, qf, kf, vf, logf


def diff_core(q, k, v, qpos, kpos, slopes, lam):
    s = jnp.einsum('btkgcd,bskcd->bkgcts', q, k, preferred_element_type=jnp.float32) * DIFF_SCALE
    dist = (qpos[:, None] - kpos[None, :]).astype(jnp.float32)
    s = s - slopes[None, :, :, None, None, None] * dist
    s = jnp.where(dist >= 0, s, -jnp.inf)
    p = jax.nn.softmax(s, axis=-1)
    a = p[:, :, :, 0] - lam * p[:, :, :, 1]
    o = jnp.einsum('bkgts,bskd->btkgd', a.astype(v.dtype), v, preferred_element_type=jnp.float32)
    return o.astype(v.dtype)


def fox_core(q, k, v, cq, ck, qpos, kpos):
    B, T = q.shape[:2]
    S = k.shape[1]
    s = jnp.einsum('btkgd,bskd->bkgts', q, k, preferred_element_type=jnp.float32) * FOX_SCALE
    cq_h = cq.reshape(B, T, KV_B, G_B).transpose(0, 2, 3, 1)
    ck_h = ck.reshape(B, S, KV_B, G_B).transpose(0, 2, 3, 1)
    s = s + (cq_h[..., :, None] - ck_h[..., None, :])
    s = jnp.where(qpos[:, None] >= kpos[None, :], s, -jnp.inf)
    p = jax.nn.softmax(s, axis=-1)
    o = jnp.einsum('bkgts,bskd->btkgd', p.astype(v.dtype), v, preferred_element_type=jnp.float32)
    return o.astype(v.dtype)


def even_attend_prompt(qd, kd, vd, qf, kf, vf, logf, lam, slopes):
    S = qd.shape[1]
    kpos = jnp.arange(S, dtype=jnp.int32)
    cum = lax.cumsum(logf, axis=1)

    def block(start):
        sl = lambda a: lax.dynamic_slice_in_dim(a, start, Q_BLOCK, axis=1)
        qpos = start + jnp.arange(Q_BLOCK, dtype=jnp.int32)
        od = diff_core(sl(qd), kd, vd, qpos, kpos, slopes, lam)
        of = fox_core(sl(qf), kf, vf, sl(cum), cum, qpos, kpos)
        return od, of

    return sweep_query_blocks(block, S)


def even_attend_sample(qd, kd, vd, qf, kf, vf, logf, lam, slopes,
                       c_dk, c_dv, c_fk, c_fv, c_fl, li, page_table):
    T = qd.shape[1]
    P = page_table.shape[1] * PAGE_SIZE
    qpos = P + jnp.arange(T, dtype=jnp.int32)
    kpos = jnp.arange(P + T, dtype=jnp.int32)

    def one(args):
        pages, qd1, kd1, vd1, qf1, kf1, vf1, lf1 = args
        cat = lambda c, new: jnp.concatenate([gather_pages(c, li, pages).astype(new.dtype), new], axis=0)[None]
        lf_past = gather_pages(c_fl, li, pages).astype(jnp.float32)
        c_past = lf_past - lax.cumsum(lf_past, axis=0, reverse=True)
        c_new = lax.cumsum(lf1, axis=0)
        ck = jnp.concatenate([c_past, c_new], axis=0)[None]
        od = diff_core(qd1[None], cat(c_dk, kd1), cat(c_dv, vd1), qpos, kpos, slopes, lam)[0]
        of = fox_core(qf1[None], cat(c_fk, kf1), cat(c_fv, vf1), c_new[None], ck, qpos, kpos)[0]
        return od, of

    return lax.map(one, (page_table, qd, kd, vd, qf, kf, vf, logf))


def even_merge(od, of, g_subln, lam_init, w_out):
    B, T = od.shape[:2]
    od = rms_norm(od, g_subln, LN_EPS) * (1.0 - lam_init)
    o = jnp.concatenate([od.reshape(B, T, H_A * DV_A), of.reshape(B, T, H_B * D_B)], axis=-1)
    return jnp.einsum('bte,ed->btd', o, w_out)


def mla_project(x, pos, w_dq, g_qn, w_uq, w_dkv, g_kvn, w_ukv):
    cq = rms_norm(jnp.einsum('btd,dr->btr', x, w_dq), g_qn, RMS_EPS)
    q = jnp.einsum('btr,rhe->bthe', cq, w_uq)
    q_nope, q_pe = q[..., :NOPE_C], rope(q[..., NOPE_C:], pos)
    kv = jnp.einsum('btd,de->bte', x, w_dkv)
    ckv = rms_norm(kv[..., :KV_LORA], g_kvn, RMS_EPS)
    kpe = rope(kv[..., KV_LORA:], pos)
    q_lat = jnp.einsum('bthn,chn->bthc', q_nope, w_ukv[..., :NOPE_C])
    return q_lat, q_pe, ckv, kpe


def mla_core(q_lat, q_pe, ckv, kpe, qpos, kpos):
    s = (jnp.einsum('bthc,bsc->bhts', q_lat, ckv, preferred_element_type=jnp.float32)
         + jnp.einsum('bthr,bsr->bhts', q_pe, kpe, preferred_element_type=jnp.float32)) * MLA_SCALE
    s = jnp.where(qpos[:, None] >= kpos[None, :], s, -jnp.inf)
    p = jax.nn.softmax(s, axis=-1)
    o = jnp.einsum('bhts,bsc->bthc', p.astype(ckv.dtype), ckv, preferred_element_type=jnp.float32)
    return o.astype(ckv.dtype)


def mla_attend_prompt(q_lat, q_pe, ckv, kpe):
    S = q_lat.shape[1]
    kpos = jnp.arange(S, dtype=jnp.int32)

    def block(start):
        sl = lambda a: lax.dynamic_slice_in_dim(a, start, Q_BLOCK, axis=1)
        return mla_core(sl(q_lat), sl(q_pe), ckv, kpe, start + jnp.arange(Q_BLOCK, dtype=jnp.int32), kpos)

    return sweep_query_blocks(block, S)


def mla_attend_sample(q_lat, q_pe, ckv, kpe, c_ckv, c_kpe, li, page_table):
    T = q_lat.shape[1]
    P = page_table.shape[1] * PAGE_SIZE
    qpos = P + jnp.arange(T, dtype=jnp.int32)
    kpos = jnp.arange(P + T, dtype=jnp.int32)

    def one(args):
        pages, ql1, qp1, cn1, kn1 = args
        cat = lambda c, new: jnp.concatenate([gather_pages(c, li, pages).astype(new.dtype), new], axis=0)[None]
        return mla_core(ql1[None], qp1[None], cat(c_ckv, cn1), cat(c_kpe, kn1), qpos, kpos)[0]

    return lax.map(one, (page_table, q_lat, q_pe, ckv, kpe))


def mla_merge(o_lat, w_ukv, w_o):
    B, T = o_lat.shape[:2]
    o = jnp.einsum('bthc,chv->bthv', o_lat, w_ukv[..., NOPE_C:])
    return jnp.einsum('bte,ed->btd', o.reshape(B, T, H_C * V_C), w_o)


def hier_moe(x, w_rg, b_rg, w_re, b_re, w_gate, w_up, w_down):
    B, T, D = x.shape
    xt = x.reshape(-1, D)
    n = xt.shape[0]
    rows = jnp.arange(n)
    lg = jnp.einsum('nd,dg->ng', xt, w_rg, preferred_element_type=jnp.float32) + b_rg.astype(jnp.float32)
    pg = jax.nn.softmax(lg, axis=-1)
    g_sel = jnp.argmax(lg, axis=-1)
    w_grp = jnp.max(pg, axis=-1, keepdims=True)
    le = (jnp.einsum('nd,de->ne', xt, w_re, preferred_element_type=jnp.float32)
          + b_re.astype(jnp.float32)).reshape(n, N_GROUPS, EXP_PER_GROUP)
    le = le[rows, g_sel]
    top_v, top_i = lax.top_k(le, TOP_K_EXPERT)
    w_top = jax.nn.softmax(top_v, axis=-1) * w_grp
    eid = g_sel[:, None] * EXP_PER_GROUP + top_i
    gates = jnp.sum(jax.nn.one_hot(eid, N_EXPERTS, dtype=jnp.float32) * w_top[..., None], axis=1)
    h = jax.nn.silu(jnp.einsum('nd,edf->nef', xt, w_gate)) * jnp.einsum('nd,edf->nef', xt, w_up)
    y = jnp.einsum('nef,efd->nd', h * gates[..., None].astype(h.dtype), w_down)
    return y.reshape(B, T, D)


def setup_inputs(seed: int = 0) -> dict:
    key = jax.random.key(seed)
    ks = iter(jax.random.split(key, 64))
    nrm = lambda shape, scale: jax.random.normal(next(ks), shape, jnp.float32) * scale
    gain = lambda shape: 1.0 + nrm(shape, 0.02)
    n_pages = PAST_LEN // PAGE_SIZE
    n_used = DEC_BATCH * n_pages
    n_pool = n_used + n_used // 4
    page_table = jax.random.permutation(next(ks), n_pool)[:n_used].reshape(DEC_BATCH, n_pages).astype(jnp.int32)
    D = D_MODEL
    return {
        'x_prompt': nrm((BATCH, SEQ, D), 1.0),
        'x_sample': nrm((DEC_BATCH, DEC_SEQ, D), 1.0),
        'cache_diff_k': nrm((N_EVEN, n_pool, PAGE_SIZE, KV_A, 2, DQK_A), 1.0),
        'cache_diff_v': nrm((N_EVEN, n_pool, PAGE_SIZE, KV_A, DV_A), 1.0),
        'cache_fox_k': nrm((N_EVEN, n_pool, PAGE_SIZE, KV_B, D_B), 1.0),
        'cache_fox_v': nrm((N_EVEN, n_pool, PAGE_SIZE, KV_B, D_B), 1.0),
        'cache_fox_logf': jax.nn.log_sigmoid(FORGET_BIAS_INIT + nrm((N_EVEN, n_pool, PAGE_SIZE, H_B), 1.0)),
        'cache_mla_ckv': nrm((N_ODD, n_pool, PAGE_SIZE, KV_LORA), 1.0),
        'cache_mla_kpe': nrm((N_ODD, n_pool, PAGE_SIZE, ROPE_C), 1.0),
        'page_table': page_table,
        'w_in_even': nrm((N_EVEN, D, IN_EVEN), D ** -0.5),
        'b_forget': FORGET_BIAS_INIT + nrm((N_EVEN, H_B), 0.1),
        'lambda_q1': nrm((N_EVEN, DQK_A), 0.1),
        'lambda_k1': nrm((N_EVEN, DQK_A), 0.1),
        'lambda_q2': nrm((N_EVEN, DQK_A), 0.1),
        'lambda_k2': nrm((N_EVEN, DQK_A), 0.1),
        'g_subln': gain((N_EVEN, DV_A)),
        'w_out_even': nrm((N_EVEN, W_EVEN, D), W_EVEN ** -0.5 * BETA),
        'w_dq': nrm((N_ODD, D, Q_LORA), D ** -0.5),
        'g_q_norm': gain((N_ODD, Q_LORA)),
        'w_uq': nrm((N_ODD, Q_LORA, H_C, NOPE_C + ROPE_C), Q_LORA ** -0.5),
        'w_dkv': nrm((N_ODD, D, KV_LORA + ROPE_C), D ** -0.5),
        'g_kv_norm': gain((N_ODD, KV_LORA)),
        'w_ukv': nrm((N_ODD, KV_LORA, H_C, NOPE_C + V_C), KV_LORA ** -0.5),
        'w_o_mla': nrm((N_ODD, H_C * V_C, D), (H_C * V_C) ** -0.5 * BETA),
        'ln_attn_g': gain((DEPTH, D)),
        'ln_attn_b': nrm((DEPTH, D), 0.02),
        'ln_ffn_g': gain((DEPTH, D)),
        'ln_ffn_b': nrm((DEPTH, D), 0.02),
        'w_router_group': nrm((DEPTH, D, N_GROUPS), D ** -0.5),
        'b_router_group': nrm((DEPTH, N_GROUPS), 0.01),
        'w_router_expert': nrm((DEPTH, D, N_EXPERTS), D ** -0.5),
        'b_router_expert': nrm((DEPTH, N_EXPERTS), 0.01),
        'w_gate': nrm((DEPTH, N_EXPERTS, D, D_EXPERT), D ** -0.5),
        'w_up': nrm((DEPTH, N_EXPERTS, D, D_EXPERT), D ** -0.5),
        'w_down': nrm((DEPTH, N_EXPERTS, D_EXPERT, D), D_EXPERT ** -0.5 * BETA),
    }


def reference(x_prompt, x_sample, cache_diff_k, cache_diff_v, cache_fox_k, cache_fox_v, cache_fox_logf,
              cache_mla_ckv, cache_mla_kpe, page_table,
              w_in_even, b_forget, lambda_q1, lambda_k1, lambda_q2, lambda_k2, g_subln, w_out_even,
              w_dq, g_q_norm, w_uq, w_dkv, g_kv_norm, w_ukv, w_o_mla,
              ln_attn_g, ln_attn_b, ln_ffn_g, ln_ffn_b,
              w_router_group, b_router_group, w_router_expert, b_router_expert, w_gate, w_up, w_down):
    S = x_prompt.shape[1]
    T = x_sample.shape[1]
    P = page_table.shape[1] * PAGE_SIZE
    pos_p = jnp.arange(S, dtype=jnp.int32)
    pos_s = P + jnp.arange(T, dtype=jnp.int32)
    slopes = alibi_slopes()
    dk_p, dv_p, fk_p, fv_p, fl_p, ck_p, kp_p = [], [], [], [], [], [], []
    dk_s, dv_s, fk_s, fv_s, fl_s, ck_s, kp_s = [], [], [], [], [], [], []
    xp, xs = x_prompt, x_sample
    for l in range(DEPTH):
        i = l // 2
        if l % 2 == 0:
            lam_init = lambda_init(l)
            lam = diff_lambda(lambda_q1[i], lambda_k1[i], lambda_q2[i], lambda_k2[i], lam_init)
            qd, kd, vd, qf, kf, vf, lf = even_project(xp, w_in_even[i], b_forget[i])
            od, of = even_attend_prompt(qd, kd, vd, qf, kf, vf, lf, lam, slopes)
            hp = even_merge(od, of, g_subln[i], lam_init, w_out_even[i])
            dk_p.append(kd); dv_p.append(vd); fk_p.append(kf); fv_p.append(vf); fl_p.append(lf.astype(xp.dtype))
            qd, kd, vd, qf, kf, vf, lf = even_project(xs, w_in_even[i], b_forget[i])
            od, of = even_attend_sample(qd, kd, vd, qf, kf, vf, lf, lam, slopes,
                                        cache_diff_k, cache_diff_v, cache_fox_k, cache_fox_v, cache_fox_logf,
                                        i, page_table)
            hs = even_merge(od, of, g_subln[i], lam_init, w_out_even[i])
            dk_s.append(kd); dv_s.append(vd); fk_s.append(kf); fv_s.append(vf); fl_s.append(lf.astype(xs.dtype))
        else:
            ql, qpe, ckv, kpe = mla_project(xp, pos_p, w_dq[i], g_q_norm[i], w_uq[i], w_dkv[i], g_kv_norm[i], w_ukv[i])
            hp = mla_merge(mla_attend_prompt(ql, qpe, ckv, kpe), w_ukv[i], w_o_mla[i])
            ck_p.append(ckv); kp_p.append(kpe)
            ql, qpe, ckv, kpe = mla_project(xs, pos_s, w_dq[i], g_q_norm[i], w_uq[i], w_dkv[i], g_kv_norm[i], w_ukv[i])
            hs = mla_merge(mla_attend_sample(ql, qpe, ckv, kpe, cache_mla_ckv, cache_mla_kpe, i, page_table),
                           w_ukv[i], w_o_mla[i])
            ck_s.append(ckv); kp_s.append(kpe)
        xp = layer_norm(ALPHA * xp + hp, ln_attn_g[l], ln_attn_b[l])
        xs = layer_norm(ALPHA * xs + hs, ln_attn_g[l], ln_attn_b[l])
        moe_w = (w_router_group[l], b_router_group[l], w_router_expert[l], b_router_expert[l],
                 w_gate[l], w_up[l], w_down[l])
        xp = layer_norm(ALPHA * xp + hier_moe(xp, *moe_w), ln_ffn_g[l], ln_ffn_b[l])
        xs = layer_norm(ALPHA * xs + hier_moe(xs, *moe_w), ln_ffn_g[l], ln_ffn_b[l])
    return (xp, xs,
            jnp.stack(dk_p), jnp.stack(dv_p), jnp.stack(fk_p), jnp.stack(fv_p), jnp.stack(fl_p),
            jnp.stack(ck_p), jnp.stack(kp_p),
            jnp.stack(dk_s), jnp.stack(dv_s), jnp.stack(fk_s), jnp.stack(fv_s), jnp.stack(fl_s),
            jnp.stack(ck_s), jnp.stack(kp_s))
```

```python
import functools
import math

import numpy as np
import jax
import jax.numpy as jnp
from jax import lax
from jax.experimental import pallas as pl
from jax.experimental.pallas import tpu as pltpu

F32 = jnp.float32
BF16 = jnp.bfloat16

D_MODEL = 2048
PAGE_SIZE = 128
Q_BLOCK = 128
H_A, KV_A, DQK_A = 8, 2, 64
G_A = H_A // KV_A
DV_A = 2 * DQK_A
H_B, KV_B, D_B = 8, 2, 128
G_B = H_B // KV_B
H_C, NOPE_C, ROPE_C, V_C = 16, 128, 64, 128
Q_LORA, KV_LORA = 512, 256
ROPE_THETA = 10000.0
N_GROUPS, EXP_PER_GROUP = 4, 8
N_EXPERTS = N_GROUPS * EXP_PER_GROUP
D_EXPERT = 512
DEPTH = 2
ALPHA = (2 * DEPTH) ** 0.25
DIFF_SCALE = DQK_A ** -0.5
FOX_SCALE = D_B ** -0.5
MLA_SCALE = (NOPE_C + ROPE_C) ** -0.5
LN_EPS = 1e-5
RMS_EPS = 1e-6

OFF_QD = 0
OFF_KD = H_A * 2 * DQK_A
OFF_VD = OFF_KD + KV_A * 2 * DQK_A
OFF_QF = OFF_VD + KV_A * DV_A
OFF_KF = OFF_QF + H_B * D_B
OFF_VF = OFF_KF + KV_B * D_B
OFF_FL = OFF_VF + KV_B * D_B
LANE = 128
VMEM_LIMIT = 56 * 2 ** 20

NT_DIMS = (((1,), (1,)), ((), ()))
NEG_INF = float("-inf")


def _cparams(sem):
    return pltpu.CompilerParams(dimension_semantics=sem, vmem_limit_bytes=VMEM_LIMIT)


def _dot(a, b):
    return jnp.dot(a, b, preferred_element_type=F32)


def _dot_nt(a, b):
    return lax.dot_general(a, b, NT_DIMS, preferred_element_type=F32)


def _split3(v):
    hi = v.astype(BF16)
    r1 = v - hi.astype(F32)
    mid = r1.astype(BF16)
    lo = (r1 - mid.astype(F32)).astype(BF16)
    return hi, mid, lo


def _mm_body(x_ref, w_ref, o_ref, *, nt):
    x = x_ref[...].astype(BF16)
    w = w_ref[...].astype(BF16)
    acc = _dot_nt(x, w) if nt else _dot(x, w)
    o_ref[...] = acc.astype(o_ref.dtype)


def matmul(x, w, *, nt=False, n_out=None, tm=512, tn=512, out_dtype=F32, name="matmul"):
    m, k = x.shape
    n = n_out if n_out is not None else (w.shape[0] if nt else w.shape[1])
    tm, tn = min(tm, m), min(tn, n)
    assert m % tm == 0 and n % tn == 0
    w_spec = pl.BlockSpec((tn, k), lambda i, j: (j, 0)) if nt else pl.BlockSpec((k, tn), lambda i, j: (0, j))
    return pl.pallas_call(
        functools.partial(_mm_body, nt=nt),
        grid=(m // tm, n // tn),
        in_specs=[pl.BlockSpec((tm, k), lambda i, j: (i, 0)), w_spec],
        out_specs=pl.BlockSpec((tm, tn), lambda i, j: (i, j)),
        out_shape=jax.ShapeDtypeStruct((m, n), out_dtype),
        compiler_params=_cparams(("parallel", "parallel")),
        name=name,
    )(x, w)


def _layer_norm(v, g, b):
    mu = jnp.mean(v, axis=-1, keepdims=True)
    d = v - mu
    var = jnp.mean(d * d, axis=-1, keepdims=True)
    return d * lax.rsqrt(var + LN_EPS) * g + b


def _mm_ln_body(x_ref, w_ref, r_ref, g_ref, b_ref, o_ref, acc_ref):
    k = pl.program_id(1)

    @pl.when(k == 0)
    def _():
        acc_ref[...] = jnp.zeros_like(acc_ref)

    acc_ref[...] += _dot(x_ref[...].astype(BF16), w_ref[...].astype(BF16))

    @pl.when(k == pl.num_programs(1) - 1)
    def _():
        o_ref[...] = _layer_norm(ALPHA * r_ref[...] + acc_ref[...], g_ref[...], b_ref[...])


def matmul_ln(x, w, resid, g, b, *, tm=512, tk=512, name="proj_ln"):
    m, kdim = x.shape
    n = w.shape[1]
    tm = min(tm, m)
    return pl.pallas_call(
        _mm_ln_body,
        grid=(m // tm, kdim // tk),
        in_specs=[pl.BlockSpec((tm, tk), lambda i, k: (i, k)), pl.BlockSpec((tk, n), lambda i, k: (k, 0)),
                  pl.BlockSpec((tm, n), lambda i, k: (i, 0)), pl.BlockSpec((1, n), lambda i, k: (0, 0)),
                  pl.BlockSpec((1, n), lambda i, k: (0, 0))],
        out_specs=pl.BlockSpec((tm, n), lambda i, k: (i, 0)),
        out_shape=jax.ShapeDtypeStruct((m, n), F32),
        scratch_shapes=[pltpu.VMEM((tm, n), F32)],
        compiler_params=_cparams(("parallel", "arbitrary")),
        name=name,
    )(x, w, resid, g, b)


def _logsig(z):
    return jnp.minimum(z, 0.0) - jnp.log1p(jnp.exp(-jnp.abs(z)))


def _forget_body(x_ref, wf_ref, wft_ref, b_ref, bt_ref, lf_ref, cum_ref, cumt_ref, carry_ref, carryt_ref, *, tm, seg):
    i = pl.program_id(0)

    @pl.when((i * tm) % seg == 0)
    def _():
        carry_ref[...] = jnp.zeros_like(carry_ref)
        carryt_ref[...] = jnp.zeros_like(carryt_ref)

    xb = x_ref[...].astype(BF16)
    lf = _logsig(_dot(xb, wf_ref[...].astype(BF16)) + b_ref[...])
    lft = _logsig(_dot_nt(wft_ref[...].astype(BF16), xb) + bt_ref[...])
    lf_ref[...] = lf
    r = lax.broadcasted_iota(jnp.int32, (tm, tm), 0)
    c = lax.broadcasted_iota(jnp.int32, (tm, tm), 1)
    same = (r // seg) == (c // seg) if seg < tm else (r >= 0)
    lower = jnp.where((c <= r) & same, 1.0, 0.0).astype(BF16)
    upper = jnp.where((r <= c) & same, 1.0, 0.0).astype(BF16)
    cum = carry_ref[0:1, :]
    for part in _split3(lf):
        cum = cum + _dot(lower, part)
    cumt = carryt_ref[:, 0:1]
    for part in _split3(lft):
        cumt = cumt + _dot(part, upper)
    cum_ref[...] = cum
    cumt_ref[...] = cumt
    carry_ref[...] = jnp.broadcast_to(cum[tm - 1:tm, :], carry_ref.shape)
    carryt_ref[...] = jnp.broadcast_to(cumt[:, tm - 1:tm], carryt_ref.shape)


def forget_gates(x, wf_pad, wft, b_row, b_col, *, seg, tm=256):
    m, k = x.shape
    tm = min(tm, m)
    assert m % tm == 0 and (seg % tm == 0 or tm % seg == 0)
    fixed = lambda i: (0, 0)
    return pl.pallas_call(
        functools.partial(_forget_body, tm=tm, seg=seg),
        grid=(m // tm,),
        in_specs=[pl.BlockSpec((tm, k), lambda i: (i, 0)), pl.BlockSpec((k, LANE), fixed), pl.BlockSpec((H_B, k), fixed),
                  pl.BlockSpec((1, LANE), fixed), pl.BlockSpec((H_B, 1), fixed)],
        out_specs=[pl.BlockSpec((tm, LANE), lambda i: (i, 0)), pl.BlockSpec((tm, LANE), lambda i: (i, 0)),
                   pl.BlockSpec((H_B, tm), lambda i: (0, i))],
        out_shape=[jax.ShapeDtypeStruct((m, LANE), F32), jax.ShapeDtypeStruct((m, LANE), F32),
                   jax.ShapeDtypeStruct((H_B, m), F32)],
        scratch_shapes=[pltpu.VMEM((8, LANE), F32), pltpu.VMEM((H_B, LANE), F32)],
        compiler_params=_cparams(("arbitrary",)),
        name="forget_gates",
    )(x, wf_pad, wft, b_row, b_col)


def _add_ln_body(x_ref, h_ref, g_ref, b_ref, o_ref):
    o_ref[...] = _layer_norm(ALPHA * x_ref[...] + h_ref[...], g_ref[...], b_ref[...])


def add_ln(x, h, g, b, *, tm=256):
    m, n = x.shape
    tm = min(tm, m)
    row = lambda i: (i, 0)
    fixed = lambda i: (0, 0)
    return pl.pallas_call(
        _add_ln_body,
        grid=(m // tm,),
        in_specs=[pl.BlockSpec((tm, n), row), pl.BlockSpec((tm, n), row), pl.BlockSpec((1, n), fixed), pl.BlockSpec((1, n), fixed)],
        out_specs=pl.BlockSpec((tm, n), row),
        out_shape=jax.ShapeDtypeStruct((m, n), F32),
        compiler_params=_cparams(("parallel",)),
        name="add_ln",
    )(x, h, g, b)


def _online_update(s, v_bf, m_ref, l_ref, acc_ref, idx=None):
    sel = (lambda r: r) if idx is None else (lambda r: r.at[idx])
    m_prev = sel(m_ref)[...]
    m_new = jnp.maximum(m_prev, jnp.max(s, axis=-1, keepdims=True))
    alpha = jnp.exp(m_prev - m_new)
    p = jnp.exp(s - m_new)
    sel(l_ref)[...] = alpha * sel(l_ref)[...] + jnp.sum(p, axis=-1, keepdims=True)
    sel(acc_ref)[...] = alpha * sel(acc_ref)[...] + _dot(p.astype(BF16), v_bf)
    sel(m_ref)[...] = m_new


def _diff_lambda(lq1, lk1, lq2, lk2, lam_init):
    s1 = jnp.sum(lq1 * lk1, axis=-1, keepdims=True)
    s2 = jnp.sum(lq2 * lk2, axis=-1, keepdims=True)
    return jnp.exp(s1) - jnp.exp(s2) + lam_init


def _subln(o, g, lam_init):
    ms = jnp.mean(o * o, axis=-1, keepdims=True)
    return o * lax.rsqrt(ms + LN_EPS) * g * (1.0 - lam_init)


def _pair_tables(n_tiles):
    qi = np.array([i for i in range(n_tiles) for _ in range(i + 1)], np.int32)
    kj = np.array([j for i in range(n_tiles) for j in range(i + 1)], np.int32)
    return jnp.asarray(qi), jnp.asarray(kj)


def _diff_prompt_body(qi_ref, kj_ref, sl_ref, q_ref, k_ref, v_ref, lq1, lk1, lq2, lk2, g_ref, o_ref,
                      m_ref, l_ref, acc_ref, *, t, lam_init):
    h = pl.program_id(0)
    n = pl.program_id(1)
    i = qi_ref[n]
    j = kj_ref[n]

    @pl.when(j == 0)
    def _():
        m_ref[...] = jnp.full_like(m_ref, NEG_INF)
        l_ref[...] = jnp.zeros_like(l_ref)
        acc_ref[...] = jnp.zeros_like(acc_ref)

    q = q_ref[...] * DIFF_SCALE
    lane = lax.broadcasted_iota(jnp.int32, q.shape, 1)
    kb = k_ref[...].astype(BF16)
    vb = v_ref[...].astype(BF16)
    row = i * t + lax.broadcasted_iota(jnp.int32, (t, t), 0)
    col = j * t + lax.broadcasted_iota(jnp.int32, (t, t), 1)
    dist = (row - col).astype(F32)
    bias = jnp.where(dist >= 0, -sl_ref[h] * dist, NEG_INF)
    for c in range(2):
        qc = jnp.where((lane >= c * DQK_A) & (lane < (c + 1) * DQK_A), q, 0.0).astype(BF16)
        _online_update(_dot_nt(qc, kb) + bias, vb, m_ref, l_ref, acc_ref, idx=c)

    @pl.when(j == i)
    def _():
        lam = _diff_lambda(lq1[...], lk1[...], lq2[...], lk2[...], lam_init)
        o = acc_ref[0] / l_ref[0] - lam * (acc_ref[1] / l_ref[1])
        o_ref[...] = _subln(o, g_ref[...], lam_init)


def diff_prompt(h, slopes, lq1, lk1, lq2, lk2, g_subln, lam_init, *, t=512):
    s = h.shape[0]
    t = min(t, s)
    qi, kj = _pair_tables(s // t)
    cb = lambda off: off // LANE
    small = lambda hh, n, qi, kj, sl: (0, 0)
    return pl.pallas_call(
        functools.partial(_diff_prompt_body, t=t, lam_init=lam_init),
        grid_spec=pltpu.PrefetchScalarGridSpec(
            num_scalar_prefetch=3,
            grid=(H_A, qi.shape[0]),
            in_specs=[pl.BlockSpec((t, LANE), lambda hh, n, qi, kj, sl: (qi[n], cb(OFF_QD) + hh)),
                      pl.BlockSpec((t, LANE), lambda hh, n, qi, kj, sl: (kj[n], cb(OFF_KD) + hh // G_A)),
                      pl.BlockSpec((t, LANE), lambda hh, n, qi, kj, sl: (kj[n], cb(OFF_VD) + hh // G_A)),
                      pl.BlockSpec((1, DQK_A), small), pl.BlockSpec((1, DQK_A), small),
                      pl.BlockSpec((1, DQK_A), small), pl.BlockSpec((1, DQK_A), small),
                      pl.BlockSpec((1, DV_A), small)],
            out_specs=pl.BlockSpec((t, LANE), lambda hh, n, qi, kj, sl: (qi[n], hh)),
            scratch_shapes=[pltpu.VMEM((2, t, 1), F32), pltpu.VMEM((2, t, 1), F32), pltpu.VMEM((2, t, DV_A), F32)]),
        out_shape=jax.ShapeDtypeStruct((s, H_A * DV_A), F32),
        compiler_params=_cparams(("parallel", "arbitrary")),
        name="diff_prompt",
    )(qi, kj, slopes, h, h, h, lq1, lk1, lq2, lk2, g_subln)


def _fox_prompt_body(qi_ref, kj_ref, q_ref, k_ref, v_ref, cq_ref, ck_ref, o_ref, m_ref, l_ref, acc_ref, *, t):
    h = pl.program_id(0)
    n = pl.program_id(1)
    i = qi_ref[n]
    j = kj_ref[n]

    @pl.when(j == 0)
    def _():
        m_ref[...] = jnp.full_like(m_ref, NEG_INF)
        l_ref[...] = jnp.zeros_like(l_ref)
        acc_ref[...] = jnp.zeros_like(acc_ref)

    qb = (q_ref[...] * FOX_SCALE).astype(BF16)
    kb = k_ref[...].astype(BF16)
    vb = v_ref[...].astype(BF16)
    lane = lax.broadcasted_iota(jnp.int32, (t, LANE), 1)
    cq = jnp.sum(jnp.where(lane == h, cq_ref[...], 0.0), axis=-1, keepdims=True)
    ck = ck_ref[...]
    row = i * t + lax.broadcasted_iota(jnp.int32, (t, t), 0)
    col = j * t + lax.broadcasted_iota(jnp.int32, (t, t), 1)
    s = _dot_nt(qb, kb) + (cq - ck)
    s = jnp.where(row >= col, s, NEG_INF)
    _online_update(s, vb, m_ref, l_ref, acc_ref)

    @pl.when(j == i)
    def _():
        o_ref[...] = acc_ref[...] / l_ref[...]


def fox_prompt(h, cum, cumt3, *, t=512):
    s = h.shape[0]
    t = min(t, s)
    qi, kj = _pair_tables(s // t)
    cb = lambda off: off // LANE
    return pl.pallas_call(
        functools.partial(_fox_prompt_body, t=t),
        grid_spec=pltpu.PrefetchScalarGridSpec(
            num_scalar_prefetch=2,
            grid=(H_B, qi.shape[0]),
            in_specs=[pl.BlockSpec((t, LANE), lambda hh, n, qi, kj: (qi[n], cb(OFF_QF) + hh)),
                      pl.BlockSpec((t, LANE), lambda hh, n, qi, kj: (kj[n], cb(OFF_KF) + hh // G_B)),
                      pl.BlockSpec((t, LANE), lambda hh, n, qi, kj: (kj[n], cb(OFF_VF) + hh // G_B)),
                      pl.BlockSpec((t, LANE), lambda hh, n, qi, kj: (qi[n], 0)),
                      pl.BlockSpec((None, 1, t), lambda hh, n, qi, kj: (hh, 0, kj[n]))],
            out_specs=pl.BlockSpec((t, LANE), lambda hh, n, qi, kj: (qi[n], hh)),
            scratch_shapes=[pltpu.VMEM((t, 1), F32), pltpu.VMEM((t, 1), F32), pltpu.VMEM((t, D_B), F32)]),
        out_shape=jax.ShapeDtypeStruct((s, H_B * D_B), F32),
        compiler_params=_cparams(("parallel", "arbitrary")),
        name="fox_prompt",
    )(qi, kj, h, h, h, cum, cumt3)


def _mla_prompt_body(qi_ref, kj_ref, q_ref, ckv_ref, kpe_ref, o_ref, m_ref, l_ref, acc_ref, *, t):
    n = pl.program_id(1)
    i = qi_ref[n]
    j = kj_ref[n]

    @pl.when(j == 0)
    def _():
        m_ref[...] = jnp.full_like(m_ref, NEG_INF)
        l_ref[...] = jnp.zeros_like(l_ref)
        acc_ref[...] = jnp.zeros_like(acc_ref)

    q = q_ref[...] * MLA_SCALE
    ckv = ckv_ref[...].astype(BF16)
    s = _dot_nt(q[:, :KV_LORA].astype(BF16), ckv) + _dot_nt(q[:, KV_LORA:].astype(BF16), kpe_ref[...].astype(BF16))
    row = i * t + lax.broadcasted_iota(jnp.int32, (t, t), 0)
    col = j * t + lax.broadcasted_iota(jnp.int32, (t, t), 1)
    s = jnp.where(row >= col, s, NEG_INF)
    _online_update(s, ckv, m_ref, l_ref, acc_ref)

    @pl.when(j == i)
    def _():
        o_ref[...] = acc_ref[...] / l_ref[...]


def mla_prompt(q3, ckv, kpe, *, t=512):
    s = ckv.shape[0]
    t = min(t, s)
    qi, kj = _pair_tables(s // t)
    return pl.pallas_call(
        functools.partial(_mla_prompt_body, t=t),
        grid_spec=pltpu.PrefetchScalarGridSpec(
            num_scalar_prefetch=2,
            grid=(H_C, qi.shape[0]),
            in_specs=[pl.BlockSpec((None, t, KV_LORA + ROPE_C), lambda hh, n, qi, kj: (hh, qi[n], 0)),
                      pl.BlockSpec((t, KV_LORA), lambda hh, n, qi, kj: (kj[n], 0)),
                      pl.BlockSpec((t, ROPE_C), lambda hh, n, qi, kj: (kj[n], 0))],
            out_specs=pl.BlockSpec((None, t, KV_LORA), lambda hh, n, qi, kj: (hh, qi[n], 0)),
            scratch_shapes=[pltpu.VMEM((t, 1), F32), pltpu.VMEM((t, 1), F32), pltpu.VMEM((t, KV_LORA), F32)]),
        out_shape=jax.ShapeDtypeStruct((H_C, s, KV_LORA), F32),
        compiler_params=_cparams(("parallel", "arbitrary")),
        name="mla_prompt",
    )(qi, kj, q3, ckv, kpe)


CHUNK_PAGES = 8
CHUNK = CHUNK_PAGES * PAGE_SIZE
T_S = 8


def _chunk_schedule(pt_ref, b, k, n_chunks, n_seq):
    c = n_chunks - 1 - k
    last = k == n_chunks - 1
    nb = jnp.where(last, b + 1, b)
    nc = jnp.where(last, n_chunks - 1, c - 1)
    return c, nb, nc, jnp.logical_or(jnp.logical_not(last), b + 1 < n_seq)


def _even_copies(pt_ref, caches, bufs, sem, b, c, slot):
    out = []
    for p in range(CHUNK_PAGES):
        page = pt_ref[b, c * CHUNK_PAGES + p]
        for a, (src, dst) in enumerate(zip(caches, bufs)):
            out.append(pltpu.make_async_copy(src.at[page], dst.at[slot, p], sem.at[slot, a]))
    return out


def _even_sample_body(pt_ref, hs_ref, cs_ref, cst_ref, lq1, lk1, lq2, lk2, g_ref,
                      dk_hbm, dv_hbm, fk_hbm, fv_hbm, lf_hbm, od_ref, of_ref,
                      dkbuf, dvbuf, fkbuf, fvbuf, lfbuf, sem, ktb, vdb, kfb, vfb,
                      md, ld, accd, mf, lff, accf, run_ref, *, n_chunks, n_seq, past_len, lam_init):
    b = pl.program_id(0)
    caches = (dk_hbm, dv_hbm, fk_hbm, fv_hbm, lf_hbm)
    bufs = (dkbuf, dvbuf, fkbuf, fvbuf, lfbuf)
    n_d, n_f = 2 * KV_A * G_A * T_S, KV_B * G_B * T_S

    @pl.when(b == 0)
    def _():
        for cp in _even_copies(pt_ref, caches, bufs, sem, 0, n_chunks - 1, 0):
            cp.start()

    hs = hs_ref[...]
    lane = lax.broadcasted_iota(jnp.int32, (T_S, LANE), 1)
    zero = jnp.zeros((T_S, LANE), F32)
    rows = []
    for kv in range(KV_A):
        for c in range(2):
            for g in range(G_A):
                qh = hs[:, OFF_QD + (kv * G_A + g) * LANE:OFF_QD + (kv * G_A + g + 1) * LANE] * DIFF_SCALE
                qh = jnp.where((lane >= c * DQK_A) & (lane < (c + 1) * DQK_A), qh, 0.0)
                rows.append(jnp.concatenate([qh, zero] if kv == 0 else [zero, qh], axis=1))
    qd = jnp.concatenate(rows, axis=0).astype(BF16)
    rows = []
    for hh in range(H_B):
        qh = hs[:, OFF_QF + hh * LANE:OFF_QF + (hh + 1) * LANE] * FOX_SCALE
        rows.append(jnp.concatenate([qh, zero] if hh // G_B == 0 else [zero, qh], axis=1))
    qf = jnp.concatenate(rows, axis=0).astype(BF16)

    rd = lax.broadcasted_iota(jnp.int32, (n_d, 1), 0)
    head_d = (rd // (2 * G_A * T_S)) * G_A + (rd // T_S) % G_A
    slope = jnp.exp2(-(head_d + 1).astype(F32))
    t_d = rd % T_S
    qpos_d = (past_len + t_d).astype(F32)
    rf = lax.broadcasted_iota(jnp.int32, (n_f, 1), 0)
    t_f = rf % T_S
    cs = cs_ref[...]
    cq = jnp.concatenate([cs[:, hh:hh + 1] for hh in range(H_B)], axis=0)

    md[...] = jnp.full_like(md, NEG_INF)
    ld[...] = jnp.zeros_like(ld)
    accd[...] = jnp.zeros_like(accd)
    mf[...] = jnp.full_like(mf, NEG_INF)
    lff[...] = jnp.zeros_like(lff)
    accf[...] = jnp.zeros_like(accf)
    run_ref[...] = jnp.zeros_like(run_ref)

    pad = jnp.zeros((LANE - T_S, 2 * LANE), F32)
    key = lax.broadcasted_iota(jnp.int32, (1, LANE), 1)
    kd_new = jnp.concatenate([hs[:, OFF_KD:OFF_KD + 2 * LANE], pad], axis=0).astype(BF16)
    vd_new = jnp.concatenate([hs[:, OFF_VD:OFF_VD + 2 * LANE], pad], axis=0).astype(BF16)
    kf_new = jnp.concatenate([hs[:, OFF_KF:OFF_KF + 2 * LANE], pad], axis=0).astype(BF16)
    vf_new = jnp.concatenate([hs[:, OFF_VF:OFF_VF + 2 * LANE], pad], axis=0).astype(BF16)
    s = _dot_nt(qd, kd_new) - slope * (t_d - key).astype(F32)
    _online_update(jnp.where(key <= t_d, s, NEG_INF), vd_new, md, ld, accd)
    cst = cst_ref[...]
    cst_rows = jnp.broadcast_to(cst[:, None, :], (H_B, T_S, LANE)).reshape(n_f, LANE)
    s = _dot_nt(qf, kf_new) + (cq - cst_rows)
    _online_update(jnp.where(key <= t_f, s, NEG_INF), vf_new, mf, lff, accf)

    jj = lax.broadcasted_iota(jnp.int32, (PAGE_SIZE, PAGE_SIZE), 0)
    ss = lax.broadcasted_iota(jnp.int32, (PAGE_SIZE, PAGE_SIZE), 1)
    later = jnp.where(jj > ss, 1.0, 0.0).astype(BF16)
    kidx = lax.broadcasted_iota(jnp.int32, (1, CHUNK), 1)

    def step(k, carry):
        c, nb, nc, has_next = _chunk_schedule(pt_ref, b, k, n_chunks, n_seq)
        slot = k % 2

        @pl.when(has_next)
        def _():
            for cp in _even_copies(pt_ref, caches, bufs, sem, nb, nc, 1 - slot):
                cp.start()

        for cp in _even_copies(pt_ref, caches, bufs, sem, b, c, slot):
            cp.wait()

        for p in range(CHUNK_PAGES):
            lo, hi = p * PAGE_SIZE, (p + 1) * PAGE_SIZE
            ktb[:, lo:hi] = dkbuf[slot, p].astype(BF16)
            for kv in range(2):
                sl = pl.ds(kv, PAGE_SIZE, stride=2)
                vdb[lo:hi, kv * LANE:(kv + 1) * LANE] = dvbuf[slot, p, sl, :].astype(BF16)
                kfb[lo:hi, kv * LANE:(kv + 1) * LANE] = fkbuf[slot, p, sl, :].astype(BF16)
                vfb[lo:hi, kv * LANE:(kv + 1) * LANE] = fvbuf[slot, p, sl, :].astype(BF16)

        kpos = (c * CHUNK + kidx).astype(F32)
        s = _dot(qd, ktb[...]) - slope * (qpos_d - kpos)
        _online_update(s, vdb[...], md, ld, accd)

        x = lfbuf[slot].reshape(CHUNK_PAGES * H_B, PAGE_SIZE)
        within = jnp.zeros_like(x)
        for part in _split3(x):
            within = within + _dot(part, later)
        tot = jnp.sum(x, axis=-1, keepdims=True)
        run = run_ref[:, 0:1]
        pieces = [None] * CHUNK_PAGES
        for p in reversed(range(CHUNK_PAGES)):
            after = within[p * H_B:(p + 1) * H_B] + run
            pieces[p] = jnp.broadcast_to(after[:, None, :], (H_B, T_S, PAGE_SIZE)).reshape(n_f, PAGE_SIZE)
            run = run + tot[p * H_B:(p + 1) * H_B]
        run_ref[...] = jnp.broadcast_to(run, run_ref.shape)
        s = _dot_nt(qf, kfb[...]) + (cq + jnp.concatenate(pieces, axis=1))
        _online_update(s, vfb[...], mf, lff, accf)
        return carry

    lax.fori_loop(0, n_chunks, step, 0)

    lam = _diff_lambda(lq1[...], lk1[...], lq2[...], lk2[...], lam_init)
    od = accd[...] / ld[...]
    g = g_ref[...]
    for kv in range(KV_A):
        for gg in range(G_A):
            r0 = ((kv * 2 + 0) * G_A + gg) * T_S
            r1 = ((kv * 2 + 1) * G_A + gg) * T_S
            o = od[r0:r0 + T_S, kv * LANE:(kv + 1) * LANE] - lam * od[r1:r1 + T_S, kv * LANE:(kv + 1) * LANE]
            od_ref[:, (kv * G_A + gg) * LANE:(kv * G_A + gg + 1) * LANE] = _subln(o, g, lam_init)
    of = accf[...] / lff[...]
    for hh in range(H_B):
        kv = hh // G_B
        of_ref[:, hh * LANE:(hh + 1) * LANE] = of[hh * T_S:(hh + 1) * T_S, kv * LANE:(kv + 1) * LANE]


def even_sample(page_table, hs, cs, cst, lq1, lk1, lq2, lk2, g_subln, dk_t, dv2, fk2, fv2, lf_t, lam_init):
    n_seq, n_pages = page_table.shape
    assert n_pages % (2 * CHUNK_PAGES) == 0 and hs.shape[0] == n_seq * T_S
    n_chunks = n_pages // CHUNK_PAGES
    width = hs.shape[1]
    small = lambda b, pt: (0, 0)
    anyspec = pl.BlockSpec(memory_space=pl.ANY)
    page_buf = lambda: pltpu.VMEM((2, CHUNK_PAGES, 2 * LANE, PAGE_SIZE), F32)
    n_d, n_f = 2 * KV_A * G_A * T_S, KV_B * G_B * T_S
    return pl.pallas_call(
        functools.partial(_even_sample_body, n_chunks=n_chunks, n_seq=n_seq, past_len=n_pages * PAGE_SIZE, lam_init=lam_init),
        grid_spec=pltpu.PrefetchScalarGridSpec(
            num_scalar_prefetch=1,
            grid=(n_seq,),
            in_specs=[pl.BlockSpec((T_S, width), lambda b, pt: (b, 0)),
                      pl.BlockSpec((T_S, LANE), lambda b, pt: (b, 0)),
                      pl.BlockSpec((None, H_B, LANE), lambda b, pt: (b, 0, 0)),
                      pl.BlockSpec((1, DQK_A), small), pl.BlockSpec((1, DQK_A), small),
                      pl.BlockSpec((1, DQK_A), small), pl.BlockSpec((1, DQK_A), small),
                      pl.BlockSpec((1, DV_A), small),
                      anyspec, anyspec, anyspec, anyspec, anyspec],
            out_specs=[pl.BlockSpec((T_S, H_A * DV_A), lambda b, pt: (b, 0)),
                       pl.BlockSpec((T_S, H_B * D_B), lambda b, pt: (b, 0))],
            scratch_shapes=[page_buf(), page_buf(), page_buf(), page_buf(),
                            pltpu.VMEM((2, CHUNK_PAGES, H_B, PAGE_SIZE), F32),
                            pltpu.SemaphoreType.DMA((2, 5)),
                            pltpu.VMEM((2 * LANE, CHUNK), BF16), pltpu.VMEM((CHUNK, 2 * LANE), BF16),
                            pltpu.VMEM((CHUNK, 2 * LANE), BF16), pltpu.VMEM((CHUNK, 2 * LANE), BF16),
                            pltpu.VMEM((n_d, 1), F32), pltpu.VMEM((n_d, 1), F32), pltpu.VMEM((n_d, 2 * LANE), F32),
                            pltpu.VMEM((n_f, 1), F32), pltpu.VMEM((n_f, 1), F32), pltpu.VMEM((n_f, 2 * LANE), F32),
                            pltpu.VMEM((H_B, LANE), F32)]),
        out_shape=[jax.ShapeDtypeStruct((n_seq * T_S, H_A * DV_A), F32), jax.ShapeDtypeStruct((n_seq * T_S, H_B * D_B), F32)],
        compiler_params=_cparams(("arbitrary",)),
        name="even_sample",
    )(page_table, hs, cs, cst, lq1, lk1, lq2, lk2, g_subln, dk_t, dv2, fk2, fv2, lf_t)


def _mla_copies(pt_ref, caches, bufs, sem, b, c, slot):
    out = []
    for p in range(CHUNK_PAGES):
        page = pt_ref[b, c * CHUNK_PAGES + p]
        for a, (src, dst) in enumerate(zip(caches, bufs)):
            out.append(pltpu.make_async_copy(src.at[page], dst.at[slot, p], sem.at[slot, a]))
    return out


def _mla_sample_body(pt_ref, q_ref, ckvn_ref, kpen_ref, ckv_hbm, kpe_hbm, o_ref,
                     ckvbuf, kpebuf, sem, ckvb, kpeb, m_ref, l_ref, acc_ref, *, n_chunks, n_seq):
    b = pl.program_id(0)
    caches = (ckv_hbm, kpe_hbm)
    bufs = (ckvbuf, kpebuf)
    n_r = H_C * T_S

    @pl.when(b == 0)
    def _():
        for cp in _mla_copies(pt_ref, caches, bufs, sem, 0, n_chunks - 1, 0):
            cp.start()

    q = q_ref[...].reshape(n_r, KV_LORA + ROPE_C) * MLA_SCALE
    ql = q[:, :KV_LORA].astype(BF16)
    qp = q[:, KV_LORA:].astype(BF16)
    t_r = lax.broadcasted_iota(jnp.int32, (n_r, 1), 0) % T_S
    key = lax.broadcasted_iota(jnp.int32, (1, LANE), 1)

    m_ref[...] = jnp.full_like(m_ref, NEG_INF)
    l_ref[...] = jnp.zeros_like(l_ref)
    acc_ref[...] = jnp.zeros_like(acc_ref)

    ckv_new = jnp.concatenate([ckvn_ref[...], jnp.zeros((LANE - T_S, KV_LORA), F32)], axis=0).astype(BF16)
    kpe_new = jnp.concatenate([kpen_ref[...], jnp.zeros((LANE - T_S, ROPE_C), F32)], axis=0).astype(BF16)
    s = _dot_nt(ql, ckv_new) + _dot_nt(qp, kpe_new)
    _online_update(jnp.where(key <= t_r, s, NEG_INF), ckv_new, m_ref, l_ref, acc_ref)

    def step(k, carry):
        c, nb, nc, has_next = _chunk_schedule(pt_ref, b, k, n_chunks, n_seq)
        slot = k % 2

        @pl.when(has_next)
        def _():
            for cp in _mla_copies(pt_ref, caches, bufs, sem, nb, nc, 1 - slot):
                cp.start()

        for cp in _mla_copies(pt_ref, caches, bufs, sem, b, c, slot):
            cp.wait()

        for p in range(CHUNK_PAGES):
            lo, hi = p * PAGE_SIZE, (p + 1) * PAGE_SIZE
            ckvb[lo:hi, :] = ckvbuf[slot, p].astype(BF16)
            kpeb[:, lo:hi] = kpebuf[slot, p].astype(BF16)

        kv = ckvb[...]
        s = _dot_nt(ql, kv) + _dot(qp, kpeb[...])
        _online_update(s, kv, m_ref, l_ref, acc_ref)
        return carry

    lax.fori_loop(0, n_chunks, step, 0)
    o_ref[...] = (acc_ref[...] / l_ref[...]).reshape(H_C, T_S, KV_LORA)


def mla_sample(page_table, q3, ckv_new, kpe_new, ckv_pages, kpe_t):
    n_seq, n_pages = page_table.shape
    assert n_pages % (2 * CHUNK_PAGES) == 0
    n_chunks = n_pages // CHUNK_PAGES
    anyspec = pl.BlockSpec(memory_space=pl.ANY)
    n_r = H_C * T_S
    return pl.pallas_call(
        functools.partial(_mla_sample_body, n_chunks=n_chunks, n_seq=n_seq),
        grid_spec=pltpu.PrefetchScalarGridSpec(
            num_scalar_prefetch=1,
            grid=(n_seq,),
            in_specs=[pl.BlockSpec((H_C, T_S, KV_LORA + ROPE_C), lambda b, pt: (0, b, 0)),
                      pl.BlockSpec((T_S, KV_LORA), lambda b, pt: (b, 0)),
                      pl.BlockSpec((T_S, ROPE_C), lambda b, pt: (b, 0)),
                      anyspec, anyspec],
            out_specs=pl.BlockSpec((H_C, T_S, KV_LORA), lambda b, pt: (0, b, 0)),
            scratch_shapes=[pltpu.VMEM((2, CHUNK_PAGES, PAGE_SIZE, KV_LORA), F32),
                            pltpu.VMEM((2, CHUNK_PAGES, ROPE_C, PAGE_SIZE), F32),
                            pltpu.SemaphoreType.DMA((2, 2)),
                            pltpu.VMEM((CHUNK, KV_LORA), BF16), pltpu.VMEM((ROPE_C, CHUNK), BF16),
                            pltpu.VMEM((n_r, 1), F32), pltpu.VMEM((n_r, 1), F32), pltpu.VMEM((n_r, KV_LORA), F32)]),
        out_shape=jax.ShapeDtypeStruct((H_C, n_seq * T_S, KV_LORA), F32),
        compiler_params=_cparams(("arbitrary",)),
        name="mla_sample",
    )(page_table, q3, ckv_new, kpe_new, ckv_pages, kpe_t)


def _rope(x, cos2, sin2):
    half = x.shape[-1] // 2
    swapped = jnp.concatenate([x[:, half:], x[:, :half]], axis=1)
    return x * cos2 + swapped * sin2


def _rms(v, g, eps):
    return v * lax.rsqrt(jnp.mean(v * v, axis=-1, keepdims=True) + eps) * g


def _mla_cq_body(x_ref, w_ref, g_ref, o_ref):
    cq = _dot(x_ref[...].astype(BF16), w_ref[...].astype(BF16))
    o_ref[...] = _rms(cq, g_ref[...], RMS_EPS)


def mla_cq(x, w_dq, g_qn, *, tm=512):
    m, k = x.shape
    tm = min(tm, m)
    return pl.pallas_call(
        _mla_cq_body,
        grid=(m // tm,),
        in_specs=[pl.BlockSpec((tm, k), lambda i: (i, 0)), pl.BlockSpec((k, Q_LORA), lambda i: (0, 0)),
                  pl.BlockSpec((1, Q_LORA), lambda i: (0, 0))],
        out_specs=pl.BlockSpec((tm, Q_LORA), lambda i: (i, 0)),
        out_shape=jax.ShapeDtypeStruct((m, Q_LORA), F32),
        compiler_params=_cparams(("parallel",)),
        name="mla_cq",
    )(x, w_dq, g_qn)


def _mla_q_body(cq_ref, wuq_ref, wuk_ref, cos_ref, sin_ref, o_ref):
    q = _dot_nt(cq_ref[...].astype(BF16), wuq_ref[...].astype(BF16))
    q_lat = _dot_nt(q[:, :NOPE_C].astype(BF16), wuk_ref[...].astype(BF16))
    o_ref[:, :KV_LORA] = q_lat
    o_ref[:, KV_LORA:] = _rope(q[:, NOPE_C:], cos_ref[...], sin_ref[...])


def mla_q(cq, wuq_t, w_ukv2, cos2, sin2, *, tm=512):
    m = cq.shape[0]
    tm = min(tm, m)
    return pl.pallas_call(
        _mla_q_body,
        grid=(H_C, m // tm),
        in_specs=[pl.BlockSpec((tm, Q_LORA), lambda h, i: (i, 0)),
                  pl.BlockSpec((None, NOPE_C + ROPE_C, Q_LORA), lambda h, i: (h, 0, 0)),
                  pl.BlockSpec((KV_LORA, NOPE_C), lambda h, i: (0, 2 * h)),
                  pl.BlockSpec((tm, ROPE_C), lambda h, i: (i, 0)), pl.BlockSpec((tm, ROPE_C), lambda h, i: (i, 0))],
        out_specs=pl.BlockSpec((None, tm, KV_LORA + ROPE_C), lambda h, i: (h, i, 0)),
        out_shape=jax.ShapeDtypeStruct((H_C, m, KV_LORA + ROPE_C), F32),
        compiler_params=_cparams(("parallel", "parallel")),
        name="mla_q",
    )(cq, wuq_t, w_ukv2, cos2, sin2)


def _mla_kv_body(x_ref, w_ref, g_ref, cos_ref, sin_ref, ckv_ref, kpe_ref):
    kv = _dot_nt(x_ref[...].astype(BF16), w_ref[...].astype(BF16))
    ckv_ref[...] = _rms(kv[:, :KV_LORA], g_ref[...], RMS_EPS)
    kpe_ref[...] = _rope(kv[:, KV_LORA:], cos_ref[...], sin_ref[...])


def mla_kv(x, wdkv_t, g_kvn, cos2, sin2, *, tm=512):
    m, k = x.shape
    tm = min(tm, m)
    return pl.pallas_call(
        _mla_kv_body,
        grid=(m // tm,),
        in_specs=[pl.BlockSpec((tm, k), lambda i: (i, 0)), pl.BlockSpec((KV_LORA + ROPE_C, k), lambda i: (0, 0)),
                  pl.BlockSpec((1, KV_LORA), lambda i: (0, 0)),
                  pl.BlockSpec((tm, ROPE_C), lambda i: (i, 0)), pl.BlockSpec((tm, ROPE_C), lambda i: (i, 0))],
        out_specs=[pl.BlockSpec((tm, KV_LORA), lambda i: (i, 0)), pl.BlockSpec((tm, ROPE_C), lambda i: (i, 0))],
        out_shape=[jax.ShapeDtypeStruct((m, KV_LORA), F32), jax.ShapeDtypeStruct((m, ROPE_C), F32)],
        compiler_params=_cparams(("parallel",)),
        name="mla_kv",
    )(x, wdkv_t, g_kvn, cos2, sin2)


def _mla_ov_body(o_ref, wuv_ref, out_ref):
    out_ref[...] = _dot(o_ref[...].astype(BF16), wuv_ref[...].astype(BF16))


def mla_ov(o3, w_ukv2, *, tm=512):
    m = o3.shape[1]
    tm = min(tm, m)
    return pl.pallas_call(
        _mla_ov_body,
        grid=(H_C, m // tm),
        in_specs=[pl.BlockSpec((None, tm, KV_LORA), lambda h, i: (h, i, 0)),
                  pl.BlockSpec((KV_LORA, V_C), lambda h, i: (0, 2 * h + 1))],
        out_specs=pl.BlockSpec((tm, V_C), lambda h, i: (i, h)),
        out_shape=jax.ShapeDtypeStruct((m, H_C * V_C), F32),
        compiler_params=_cparams(("parallel", "parallel")),
        name="mla_ov",
    )(o3, w_ukv2)


def _router_body(x_ref, w_ref, b_ref, gates_ref):
    x = x_ref[...]
    w = w_ref[...]
    xh = x.astype(BF16)
    xl = (x - xh.astype(F32)).astype(BF16)
    wh = w.astype(BF16)
    wl = (w - wh.astype(F32)).astype(BF16)
    logit = _dot(xh, wh) + _dot(xh, wl) + _dot(xl, wh) + b_ref[...]
    lane = lax.broadcasted_iota(jnp.int32, logit.shape, 1)
    big = jnp.int32(2 ** 30)

    def first_max(mask):
        v = jnp.max(jnp.where(mask, logit, NEG_INF), axis=-1, keepdims=True)
        idx = jnp.min(jnp.where(mask & (logit == v), lane, big), axis=-1, keepdims=True)
        return v, idx

    is_group = (lane >= N_EXPERTS) & (lane < N_EXPERTS + N_GROUPS)
    gmax, gidx = first_max(is_group)
    w_grp = 1.0 / jnp.sum(jnp.where(is_group, jnp.exp(logit - gmax), 0.0), axis=-1, keepdims=True)
    g_sel = gidx - N_EXPERTS
    in_grp = (lane >= g_sel * EXP_PER_GROUP) & (lane < (g_sel + 1) * EXP_PER_GROUP)
    v1, i1 = first_max(in_grp)
    v2, i2 = first_max(in_grp & (lane != i1))
    e2 = jnp.exp(v2 - v1)
    w1 = w_grp / (1.0 + e2)
    w2 = w_grp * e2 / (1.0 + e2)
    gates_ref[...] = jnp.where(lane == i1, w1, 0.0) + jnp.where(lane == i2, w2, 0.0)


def router(x, w_pad, b_pad, *, tm=512):
    m, k = x.shape
    tm = min(tm, m)
    return pl.pallas_call(
        _router_body,
        grid=(m // tm,),
        in_specs=[pl.BlockSpec((tm, k), lambda i: (i, 0)), pl.BlockSpec((k, LANE), lambda i: (0, 0)),
                  pl.BlockSpec((1, LANE), lambda i: (0, 0))],
        out_specs=pl.BlockSpec((tm, LANE), lambda i: (i, 0)),
        out_shape=jax.ShapeDtypeStruct((m, LANE), F32),
        compiler_params=_cparams(("parallel",)),
        name="router",
    )(x, w_pad, b_pad)


def _moe_dense_body(x_ref, gates_ref, wg_ref, wu_ref, wd_ref, o_ref):
    e = pl.program_id(1)

    @pl.when(e == 0)
    def _():
        o_ref[...] = jnp.zeros_like(o_ref)

    xb = x_ref[...].astype(BF16)
    gates = gates_ref[...]
    lane = lax.broadcasted_iota(jnp.int32, gates.shape, 1)
    gate = jnp.sum(jnp.where(lane == e, gates, 0.0), axis=-1, keepdims=True)
    a = _dot(xb, wg_ref[...].astype(BF16))
    u = _dot(xb, wu_ref[...].astype(BF16))
    hmid = a * (1.0 / (1.0 + jnp.exp(-a))) * u * gate
    o_ref[...] += _dot(hmid.astype(BF16), wd_ref[...].astype(BF16))


def moe_dense(x, gates, w_gate, w_up, w_down, layer, *, tm=256):
    m, d = x.shape
    tm = min(tm, m)
    return pl.pallas_call(
        _moe_dense_body,
        grid=(m // tm, N_EXPERTS),
        in_specs=[pl.BlockSpec((tm, d), lambda i, e: (i, 0)), pl.BlockSpec((tm, LANE), lambda i, e: (i, 0)),
                  pl.BlockSpec((None, None, d, D_EXPERT), lambda i, e: (layer, e, 0, 0)),
                  pl.BlockSpec((None, None, d, D_EXPERT), lambda i, e: (layer, e, 0, 0)),
                  pl.BlockSpec((None, None, D_EXPERT, d), lambda i, e: (layer, e, 0, 0))],
        out_specs=pl.BlockSpec((tm, d), lambda i, e: (i, 0)),
        out_shape=jax.ShapeDtypeStruct((m, d), F32),
        compiler_params=_cparams(("parallel", "arbitrary")),
        name="moe_dense",
    )(x, gates, w_gate, w_up, w_down)


def _rope_tables(pos):
    half = ROPE_C // 2
    freqs = ROPE_THETA ** (-jnp.arange(half, dtype=F32) / half)
    ang = pos.astype(F32)[:, None] * freqs
    cos, sin = jnp.cos(ang), jnp.sin(ang)
    return jnp.concatenate([cos, cos], axis=1), jnp.concatenate([-sin, sin], axis=1)


def _lambda_init(layer_idx):
    return 0.8 - 0.6 * math.exp(-0.3 * layer_idx)


def _row(v):
    return v.reshape(1, -1)


def _moe_block(x, l, w_router_group, b_router_group, w_router_expert, b_router_expert, w_gate, w_up, w_down):
    pad = jnp.zeros((D_MODEL, LANE - N_EXPERTS - N_GROUPS), F32)
    w_pad = jnp.concatenate([w_router_expert[l], w_router_group[l], pad], axis=1)
    b_pad = jnp.concatenate([b_router_expert[l], b_router_group[l], jnp.zeros((LANE - N_EXPERTS - N_GROUPS,), F32)])[None]
    gates = router(x, w_pad, b_pad)
    return moe_dense(x, gates, w_gate, w_up, w_down, l)


def kernel(x_prompt, x_sample, cache_diff_k, cache_diff_v, cache_fox_k, cache_fox_v, cache_fox_logf, cache_mla_ckv, cache_mla_kpe, page_table, w_in_even, b_forget, lambda_q1, lambda_k1, lambda_q2, lambda_k2, g_subln, w_out_even, w_dq, g_q_norm, w_uq, w_dkv, g_kv_norm, w_ukv, w_o_mla, ln_attn_g, ln_attn_b, ln_ffn_g, ln_ffn_b, w_router_group, b_router_group, w_router_expert, b_router_expert, w_gate, w_up, w_down):
    n_b, s_len, d = x_prompt.shape
    n_seq, t_new, _ = x_sample.shape
    assert n_b == 1 and t_new == T_S
    n_pool = cache_diff_k.shape[1]
    past = page_table.shape[1] * PAGE_SIZE
    xp = x_prompt.reshape(s_len, d)
    xs = x_sample.reshape(n_seq * T_S, d)
    slopes = 2.0 ** (-8.0 * jnp.arange(1, H_A + 1, dtype=F32) / H_A)
    moe_w = (w_router_group, b_router_group, w_router_expert, b_router_expert, w_gate, w_up, w_down)
    outs_p, outs_s = {}, {}

    for l in range(DEPTH):
        i = l // 2
        if l % 2 == 0:
            lam_init = _lambda_init(l)
            w_t = jnp.transpose(w_in_even[i])
            wft = w_t[OFF_FL:OFF_FL + H_B]
            wf_pad = jnp.concatenate([jnp.transpose(wft), jnp.zeros((d, LANE - H_B), F32)], axis=1)
            b_row = jnp.concatenate([b_forget[i], jnp.zeros((LANE - H_B,), F32)])[None]
            b_col = b_forget[i][:, None]
            lam_args = (_row(lambda_q1[i]), _row(lambda_k1[i]), _row(lambda_q2[i]), _row(lambda_k2[i]), _row(g_subln[i]))
            hp = matmul(xp, w_t, nt=True, n_out=OFF_FL, name="even_proj_p")
            lf_p, cum_p, cumt_p = forget_gates(xp, wf_pad, wft, b_row, b_col, seg=s_len)
            od = diff_prompt(hp, slopes, *lam_args, lam_init)
            of = fox_prompt(hp, cum_p, cumt_p.reshape(H_B, 1, s_len))
            xp = matmul_ln(jnp.concatenate([od, of], axis=1), w_out_even[i], xp,
                           _row(ln_attn_g[l]), _row(ln_attn_b[l]), name="even_out_p")
            outs_p[l] = (hp[:, OFF_KD:OFF_VD].reshape(1, s_len, KV_A, 2, DQK_A), hp[:, OFF_VD:OFF_QF].reshape(1, s_len, KV_A, DV_A),
                         hp[:, OFF_KF:OFF_VF].reshape(1, s_len, KV_B, D_B), hp[:, OFF_VF:OFF_FL].reshape(1, s_len, KV_B, D_B),
                         lf_p[:, :H_B].reshape(1, s_len, H_B))
            hs = matmul(xs, w_t, nt=True, n_out=OFF_FL, name="even_proj_s")
            lf_s, cum_s, cumt_s = forget_gates(xs, wf_pad, wft, b_row, b_col, seg=T_S)
            cst = jnp.transpose(cumt_s.reshape(H_B, n_seq, T_S), (1, 0, 2))
            cst = jnp.concatenate([cst, jnp.zeros((n_seq, H_B, LANE - T_S), F32)], axis=2)
            dk_t = jnp.transpose(cache_diff_k[i], (0, 2, 3, 4, 1)).reshape(n_pool, KV_A * 2 * DQK_A, PAGE_SIZE)
            dv2 = cache_diff_v[i].reshape(n_pool, PAGE_SIZE * KV_A, DV_A)
            fk2 = cache_fox_k[i].reshape(n_pool, PAGE_SIZE * KV_B, D_B)
            fv2 = cache_fox_v[i].reshape(n_pool, PAGE_SIZE * KV_B, D_B)
            lf_t = jnp.transpose(cache_fox_logf[i], (0, 2, 1))
            od, of = even_sample(page_table, hs, cum_s, cst, *lam_args, dk_t, dv2, fk2, fv2, lf_t, lam_init)
            xs = matmul_ln(jnp.concatenate([od, of], axis=1), w_out_even[i], xs,
                           _row(ln_attn_g[l]), _row(ln_attn_b[l]), name="even_out_s")
            outs_s[l] = (hs[:, OFF_KD:OFF_VD].reshape(n_seq, T_S, KV_A, 2, DQK_A), hs[:, OFF_VD:OFF_QF].reshape(n_seq, T_S, KV_A, DV_A),
                         hs[:, OFF_KF:OFF_VF].reshape(n_seq, T_S, KV_B, D_B), hs[:, OFF_VF:OFF_FL].reshape(n_seq, T_S, KV_B, D_B),
                         lf_s[:, :H_B].reshape(n_seq, T_S, H_B))
        else:
            wuq_t = jnp.transpose(w_uq[i], (1, 2, 0))
            wdkv_t = jnp.transpose(w_dkv[i])
            w_ukv2 = w_ukv[i].reshape(KV_LORA, H_C * (NOPE_C + V_C))
            kpe_t = jnp.transpose(cache_mla_kpe[i], (0, 2, 1))
            for grp in ("p", "s"):
                x = xp if grp == "p" else xs
                pos = jnp.arange(s_len, dtype=jnp.int32) if grp == "p" else past + jnp.tile(jnp.arange(T_S, dtype=jnp.int32), n_seq)
                cos2, sin2 = _rope_tables(pos)
                cq = mla_cq(x, w_dq[i], _row(g_q_norm[i]))
                q3 = mla_q(cq, wuq_t, w_ukv2, cos2, sin2)
                ckv, kpe = mla_kv(x, wdkv_t, _row(g_kv_norm[i]), cos2, sin2)
                if grp == "p":
                    o3 = mla_prompt(q3, ckv, kpe)
                else:
                    o3 = mla_sample(page_table, q3, ckv, kpe, cache_mla_ckv[i], kpe_t)
                ov = mla_ov(o3, w_ukv2)
                x = matmul_ln(ov, w_o_mla[i], x, _row(ln_attn_g[l]), _row(ln_attn_b[l]), name="mla_out_" + grp)
                if grp == "p":
                    xp = x
                    outs_p[l] = (ckv.reshape(1, s_len, KV_LORA), kpe.reshape(1, s_len, ROPE_C))
                else:
                    xs = x
                    outs_s[l] = (ckv.reshape(n_seq, T_S, KV_LORA), kpe.reshape(n_seq, T_S, ROPE_C))
        xp = add_ln(xp, _moe_block(xp, l, *moe_w), _row(ln_ffn_g[l]), _row(ln_ffn_b[l]))
        xs = add_ln(xs, _moe_block(xs, l, *moe_w), _row(ln_ffn_g[l]), _row(ln_ffn_b[l]))

    stack = lambda outs, k, ls: jnp.stack([outs[l][k] for l in ls])
    even, odd = range(0, DEPTH, 2), range(1, DEPTH, 2)
    return (xp.reshape(1, s_len, d), xs.reshape(n_seq, T_S, d),
            *[stack(outs_p, k, even) for k in range(5)], *[stack(outs_p, k, odd) for k in range(2)],
            *[stack(outs_s, k, even) for k in range(5)], *[stack(outs_s, k, odd) for k in range(2)])
```

```python
import functools
import math

import numpy as np
import jax
import jax.numpy as jnp
from jax import lax
from jax.experimental import pallas as pl
from jax.experimental.pallas import tpu as pltpu

F32 = jnp.float32
BF16 = jnp.bfloat16

D_MODEL = 2048
PAGE_SIZE = 128
Q_BLOCK = 128
H_A, KV_A, DQK_A = 8, 2, 64
G_A = H_A // KV_A
DV_A = 2 * DQK_A
H_B, KV_B, D_B = 8, 2, 128
G_B = H_B // KV_B
H_C, NOPE_C, ROPE_C, V_C = 16, 128, 64, 128
Q_LORA, KV_LORA = 512, 256
ROPE_THETA = 10000.0
N_GROUPS, EXP_PER_GROUP = 4, 8
N_EXPERTS = N_GROUPS * EXP_PER_GROUP
D_EXPERT = 512
DEPTH = 2
ALPHA = (2 * DEPTH) ** 0.25
DIFF_SCALE = DQK_A ** -0.5
FOX_SCALE = D_B ** -0.5
MLA_SCALE = (NOPE_C + ROPE_C) ** -0.5
LN_EPS = 1e-5
RMS_EPS = 1e-6

OFF_QD = 0
OFF_KD = H_A * 2 * DQK_A
OFF_VD = OFF_KD + KV_A * 2 * DQK_A
OFF_QF = OFF_VD + KV_A * DV_A
OFF_KF = OFF_QF + H_B * D_B
OFF_VF = OFF_KF + KV_B * D_B
OFF_FL = OFF_VF + KV_B * D_B
LANE = 128
VMEM_LIMIT = 56 * 2 ** 20

NT_DIMS = (((1,), (1,)), ((), ()))
NEG_INF = float("-inf")


def _cparams(sem):
    return pltpu.CompilerParams(dimension_semantics=sem, vmem_limit_bytes=VMEM_LIMIT)


def _dot(a, b):
    return jnp.dot(a, b, preferred_element_type=F32)


def _dot_nt(a, b):
    return lax.dot_general(a, b, NT_DIMS, preferred_element_type=F32)


def _split3(v):
    hi = v.astype(BF16)
    r1 = v - hi.astype(F32)
    mid = r1.astype(BF16)
    lo = (r1 - mid.astype(F32)).astype(BF16)
    return hi, mid, lo


def _mm_body(x_ref, w_ref, o_ref, *, nt):
    x = x_ref[...].astype(BF16)
    w = w_ref[...].astype(BF16)
    acc = _dot_nt(x, w) if nt else _dot(x, w)
    o_ref[...] = acc.astype(o_ref.dtype)


def matmul(x, w, *, nt=False, n_out=None, tm=512, tn=512, out_dtype=F32, name="matmul"):
    m, k = x.shape
    n = n_out if n_out is not None else (w.shape[0] if nt else w.shape[1])
    tm, tn = min(tm, m), min(tn, n)
    assert m % tm == 0 and n % tn == 0
    w_spec = pl.BlockSpec((tn, k), lambda i, j: (j, 0)) if nt else pl.BlockSpec((k, tn), lambda i, j: (0, j))
    return pl.pallas_call(
        functools.partial(_mm_body, nt=nt),
        grid=(m // tm, n // tn),
        in_specs=[pl.BlockSpec((tm, k), lambda i, j: (i, 0)), w_spec],
        out_specs=pl.BlockSpec((tm, tn), lambda i, j: (i, j)),
        out_shape=jax.ShapeDtypeStruct((m, n), out_dtype),
        compiler_params=_cparams(("parallel", "parallel")),
        name=name,
    )(x, w)


def _layer_norm(v, g, b):
    mu = jnp.mean(v, axis=-1, keepdims=True)
    d = v - mu
    var = jnp.mean(d * d, axis=-1, keepdims=True)
    return d * lax.rsqrt(var + LN_EPS) * g + b


def _mm_ln_body(x_ref, w_ref, r_ref, g_ref, b_ref, o_ref, acc_ref):
    k = pl.program_id(1)

    @pl.when(k == 0)
    def _():
        acc_ref[...] = jnp.zeros_like(acc_ref)

    acc_ref[...] += _dot(x_ref[...].astype(BF16), w_ref[...].astype(BF16))

    @pl.when(k == pl.num_programs(1) - 1)
    def _():
        o_ref[...] = _layer_norm(ALPHA * r_ref[...] + acc_ref[...], g_ref[...], b_ref[...])


def matmul_ln(x, w, resid, g, b, *, tm=512, tk=512, name="proj_ln"):
    m, kdim = x.shape
    n = w.shape[1]
    tm = min(tm, m)
    return pl.pallas_call(
        _mm_ln_body,
        grid=(m // tm, kdim // tk),
        in_specs=[pl.BlockSpec((tm, tk), lambda i, k: (i, k)), pl.BlockSpec((tk, n), lambda i, k: (k, 0)),
                  pl.BlockSpec((tm, n), lambda i, k: (i, 0)), pl.BlockSpec((1, n), lambda i, k: (0, 0)),
                  pl.BlockSpec((1, n), lambda i, k: (0, 0))],
        out_specs=pl.BlockSpec((tm, n), lambda i, k: (i, 0)),
        out_shape=jax.ShapeDtypeStruct((m, n), F32),
        scratch_shapes=[pltpu.VMEM((tm, n), F32)],
        compiler_params=_cparams(("parallel", "arbitrary")),
        name=name,
    )(x, w, resid, g, b)


def _logsig(z):
    return jnp.minimum(z, 0.0) - jnp.log1p(jnp.exp(-jnp.abs(z)))


def _forget_body(x_ref, wf_ref, wft_ref, b_ref, bt_ref, lf_ref, cum_ref, cumt_ref, carry_ref, carryt_ref, *, tm, seg):
    i = pl.program_id(0)

    @pl.when((i * tm) % seg == 0)
    def _():
        carry_ref[...] = jnp.zeros_like(carry_ref)
        carryt_ref[...] = jnp.zeros_like(carryt_ref)

    xb = x_ref[...].astype(BF16)
    lf = _logsig(_dot(xb, wf_ref[...].astype(BF16)) + b_ref[...])
    lft = _logsig(_dot_nt(wft_ref[...].astype(BF16), xb) + bt_ref[...])
    lf_ref[...] = lf
    r = lax.broadcasted_iota(jnp.int32, (tm, tm), 0)
    c = lax.broadcasted_iota(jnp.int32, (tm, tm), 1)
    same = (r // seg) == (c // seg) if seg < tm else (r >= 0)
    lower = jnp.where((c <= r) & same, 1.0, 0.0).astype(BF16)
    upper = jnp.where((r <= c) & same, 1.0, 0.0).astype(BF16)
    cum = carry_ref[0:1, :]
    for part in _split3(lf):
        cum = cum + _dot(lower, part)
    cumt = carryt_ref[:, 0:1]
    for part in _split3(lft):
        cumt = cumt + _dot(part, upper)
    cum_ref[...] = cum
    cumt_ref[...] = cumt
    carry_ref[...] = jnp.broadcast_to(cum[tm - 1:tm, :], carry_ref.shape)
    carryt_ref[...] = jnp.broadcast_to(cumt[:, tm - 1:tm], carryt_ref.shape)


def forget_gates(x, wf_pad, wft, b_row, b_col, *, seg, tm=256):
    m, k = x.shape
    tm = min(tm, m)
    assert m % tm == 0 and (seg % tm == 0 or tm % seg == 0)
    fixed = lambda i: (0, 0)
    return pl.pallas_call(
        functools.partial(_forget_body, tm=tm, seg=seg),
        grid=(m // tm,),
        in_specs=[pl.BlockSpec((tm, k), lambda i: (i, 0)), pl.BlockSpec((k, LANE), fixed), pl.BlockSpec((H_B, k), fixed),
                  pl.BlockSpec((1, LANE), fixed), pl.BlockSpec((H_B, 1), fixed)],
        out_specs=[pl.BlockSpec((tm, LANE), lambda i: (i, 0)), pl.BlockSpec((tm, LANE), lambda i: (i, 0)),
                   pl.BlockSpec((H_B, tm), lambda i: (0, i))],
        out_shape=[jax.ShapeDtypeStruct((m, LANE), F32), jax.ShapeDtypeStruct((m, LANE), F32),
                   jax.ShapeDtypeStruct((H_B, m), F32)],
        scratch_shapes=[pltpu.VMEM((8, LANE), F32), pltpu.VMEM((H_B, LANE), F32)],
        compiler_params=_cparams(("arbitrary",)),
        name="forget_gates",
    )(x, wf_pad, wft, b_row, b_col)


def _online_update(s, v_bf, m_ref, l_ref, acc_ref, idx=None):
    sel = (lambda r: r) if idx is None else (lambda r: r.at[idx])
    m_prev = sel(m_ref)[...]
    m_new = jnp.maximum(m_prev, jnp.max(s, axis=-1, keepdims=True))
    alpha = jnp.exp(m_prev - m_new)
    p = jnp.exp(s - m_new)
    sel(l_ref)[...] = alpha * sel(l_ref)[...] + jnp.sum(p, axis=-1, keepdims=True)
    sel(acc_ref)[...] = alpha * sel(acc_ref)[...] + _dot(p.astype(BF16), v_bf)
    sel(m_ref)[...] = m_new


def _softmax_step(s, v_bf, m_prev, acc_prev, l_prev=None):
    m_new = jnp.maximum(m_prev, jnp.max(s, axis=-1, keepdims=True))
    alpha = jnp.exp(m_prev - m_new)
    p = jnp.exp(s - m_new)
    acc_new = alpha * acc_prev + _dot(p.astype(BF16), v_bf)
    if l_prev is None:
        return m_new, acc_new
    return m_new, acc_new, alpha * l_prev + jnp.sum(p, axis=-1, keepdims=True)


def _with_ones(v_bf):
    return jnp.concatenate([v_bf, jnp.ones_like(v_bf)], axis=1)


def _diff_lambda(lq1, lk1, lq2, lk2, lam_init):
    s1 = jnp.sum(lq1 * lk1, axis=-1, keepdims=True)
    s2 = jnp.sum(lq2 * lk2, axis=-1, keepdims=True)
    return jnp.exp(s1) - jnp.exp(s2) + lam_init


def _subln(o, g, lam_init):
    ms = jnp.mean(o * o, axis=-1, keepdims=True)
    return o * lax.rsqrt(ms + LN_EPS) * g * (1.0 - lam_init)


ROW_SPLIT = 2
FOX_SPLIT = 2
MLA_SPLIT = 2


def _causal_tile(t):
    return lax.broadcasted_iota(jnp.int32, (t, t), 0) >= lax.broadcasted_iota(jnp.int32, (t, t), 1)


def _pair_tables(n_tiles):
    qi = np.array([i for i in range(n_tiles) for _ in range(i + 1)], np.int32)
    kj = np.array([j for i in range(n_tiles) for j in range(i + 1)], np.int32)
    return jnp.asarray(qi), jnp.asarray(kj)


def _diff_prompt_body(qi_ref, kj_ref, sl_ref, q_ref, k_ref, v_ref, lq1, lk1, lq2, lk2, g_ref, o_ref,
                      m_ref, acc_ref, *, t, lam_init):
    h = pl.program_id(0)
    n = pl.program_id(1)
    i = qi_ref[n]
    j = kj_ref[n]

    @pl.when(j == 0)
    def _():
        m_ref[...] = jnp.full_like(m_ref, NEG_INF)
        acc_ref[...] = jnp.zeros_like(acc_ref)

    q = q_ref[...] * DIFF_SCALE
    lane = lax.broadcasted_iota(jnp.int32, q.shape, 1)
    kb = k_ref[...].astype(BF16)
    v1 = _with_ones(v_ref[...].astype(BF16))
    kcol = lax.broadcasted_iota(jnp.int32, (1, t), 1)
    bias = sl_ref[h] * ((j - i) * t + kcol).astype(F32)

    def update(masked):
        sub = t // ROW_SPLIT
        res = []
        for c in range(2):
            qc = jnp.where((lane >= c * DQK_A) & (lane < (c + 1) * DQK_A), q, 0.0).astype(BF16)
            for r in range(ROW_SPLIT):
                rows = slice(r * sub, (r + 1) * sub)
                sc = _dot_nt(qc[rows], kb) + bias
                if masked:
                    sc = jnp.where(_causal_tile(t)[rows], sc, NEG_INF)
                res.append((c, rows, _softmax_step(sc, v1, m_ref[c, rows], acc_ref[c, rows])))
        for c, rows, (m_new, acc_new) in res:
            m_ref[c, rows] = m_new
            acc_ref[c, rows] = acc_new

    @pl.when(j < i)
    def _():
        update(False)

    @pl.when(j == i)
    def _():
        update(True)
        lam = _diff_lambda(lq1[...], lk1[...], lq2[...], lk2[...], lam_init)
        a0, a1 = acc_ref[0], acc_ref[1]
        o = a0[:, :DV_A] / a0[:, DV_A:DV_A + 1] - lam * (a1[:, :DV_A] / a1[:, DV_A:DV_A + 1])
        o_ref[...] = _subln(o, g_ref[...], lam_init)


def diff_prompt(h, slopes, lq1, lk1, lq2, lk2, g_subln, lam_init, *, t=512):
    s = h.shape[0]
    t = min(t, s)
    qi, kj = _pair_tables(s // t)
    cb = lambda off: off // LANE
    small = lambda hh, n, qi, kj, sl: (0, 0)
    return pl.pallas_call(
        functools.partial(_diff_prompt_body, t=t, lam_init=lam_init),
        grid_spec=pltpu.PrefetchScalarGridSpec(
            num_scalar_prefetch=3,
            grid=(H_A, qi.shape[0]),
            in_specs=[pl.BlockSpec((t, LANE), lambda hh, n, qi, kj, sl: (qi[n], cb(OFF_QD) + hh)),
                      pl.BlockSpec((t, LANE), lambda hh, n, qi, kj, sl: (kj[n], cb(OFF_KD) + hh // G_A)),
                      pl.BlockSpec((t, LANE), lambda hh, n, qi, kj, sl: (kj[n], cb(OFF_VD) + hh // G_A)),
                      pl.BlockSpec((1, DQK_A), small), pl.BlockSpec((1, DQK_A), small),
                      pl.BlockSpec((1, DQK_A), small), pl.BlockSpec((1, DQK_A), small),
                      pl.BlockSpec((1, DV_A), small)],
            out_specs=pl.BlockSpec((t, LANE), lambda hh, n, qi, kj, sl: (qi[n], hh)),
            scratch_shapes=[pltpu.VMEM((2, t, 1), F32), pltpu.VMEM((2, t, 2 * DV_A), F32)]),
        out_shape=jax.ShapeDtypeStruct((s, H_A * DV_A), F32),
        compiler_params=_cparams(("parallel", "arbitrary")),
        name="diff_prompt",
    )(qi, kj, slopes, h, h, h, lq1, lk1, lq2, lk2, g_subln)


def _fox_prompt_body(qi_ref, kj_ref, q_ref, k_ref, v_ref, cq_ref, ck_ref, o_ref, m_ref, acc_ref, *, t):
    h = pl.program_id(0)
    n = pl.program_id(1)
    i = qi_ref[n]
    j = kj_ref[n]

    @pl.when(j == 0)
    def _():
        m_ref[...] = jnp.full_like(m_ref, NEG_INF)
        acc_ref[...] = jnp.zeros_like(acc_ref)

    qb = (q_ref[...] * FOX_SCALE).astype(BF16)
    kb = k_ref[...].astype(BF16)
    v1 = _with_ones(v_ref[...].astype(BF16))
    lane = lax.broadcasted_iota(jnp.int32, (1, LANE), 1)
    base = jnp.sum(jnp.where(lane == h, cq_ref[0:1, :], 0.0), axis=-1, keepdims=True)
    bias = base - ck_ref[...]

    def update(masked):
        sub = t // FOX_SPLIT
        res = []
        for r in range(FOX_SPLIT):
            rows = slice(r * sub, (r + 1) * sub)
            sc = _dot_nt(qb[rows], kb) + bias
            if masked:
                sc = jnp.where(_causal_tile(t)[rows], sc, NEG_INF)
            res.append((rows, _softmax_step(sc, v1, m_ref[rows], acc_ref[rows])))
        for rows, (m_new, acc_new) in res:
            m_ref[rows] = m_new
            acc_ref[rows] = acc_new

    @pl.when(j < i)
    def _():
        update(False)

    @pl.when(j == i)
    def _():
        update(True)
        acc = acc_ref[...]
        o_ref[...] = acc[:, :D_B] / acc[:, D_B:D_B + 1]


def fox_prompt(h, cum, cumt3, *, t=512):
    s = h.shape[0]
    t = min(t, s)
    qi, kj = _pair_tables(s // t)
    cb = lambda off: off // LANE
    return pl.pallas_call(
        functools.partial(_fox_prompt_body, t=t),
        grid_spec=pltpu.PrefetchScalarGridSpec(
            num_scalar_prefetch=2,
            grid=(H_B, qi.shape[0]),
            in_specs=[pl.BlockSpec((t, LANE), lambda hh, n, qi, kj: (qi[n], cb(OFF_QF) + hh)),
                      pl.BlockSpec((t, LANE), lambda hh, n, qi, kj: (kj[n], cb(OFF_KF) + hh // G_B)),
                      pl.BlockSpec((t, LANE), lambda hh, n, qi, kj: (kj[n], cb(OFF_VF) + hh // G_B)),
                      pl.BlockSpec((t, LANE), lambda hh, n, qi, kj: (qi[n], 0)),
                      pl.BlockSpec((None, 1, t), lambda hh, n, qi, kj: (hh, 0, kj[n]))],
            out_specs=pl.BlockSpec((t, LANE), lambda hh, n, qi, kj: (qi[n], hh)),
            scratch_shapes=[pltpu.VMEM((t, 1), F32), pltpu.VMEM((t, 2 * D_B), F32)]),
        out_shape=jax.ShapeDtypeStruct((s, H_B * D_B), F32),
        compiler_params=_cparams(("parallel", "arbitrary")),
        name="fox_prompt",
    )(qi, kj, h, h, h, cum, cumt3)


def _mla_prompt_body(qi_ref, kj_ref, q_ref, ckv_ref, kpe_ref, o_ref, m_ref, l_ref, acc_ref, *, t):
    n = pl.program_id(1)
    i = qi_ref[n]
    j = kj_ref[n]

    @pl.when(j == 0)
    def _():
        m_ref[...] = jnp.full_like(m_ref, NEG_INF)
        l_ref[...] = jnp.zeros_like(l_ref)
        acc_ref[...] = jnp.zeros_like(acc_ref)

    q = q_ref[...] * MLA_SCALE
    ql = q[:, :KV_LORA].astype(BF16)
    qp = q[:, KV_LORA:].astype(BF16)
    ckv = ckv_ref[...].astype(BF16)
    kpe = kpe_ref[...].astype(BF16)

    def update(masked):
        sub = t // MLA_SPLIT
        res = []
        for r in range(MLA_SPLIT):
            rows = slice(r * sub, (r + 1) * sub)
            sc = _dot_nt(ql[rows], ckv) + _dot_nt(qp[rows], kpe)
            if masked:
                sc = jnp.where(_causal_tile(t)[rows], sc, NEG_INF)
            res.append((rows, _softmax_step(sc, ckv, m_ref[rows], acc_ref[rows], l_ref[rows])))
        for rows, (m_new, acc_new, l_new) in res:
            m_ref[rows] = m_new
            l_ref[rows] = l_new
            acc_ref[rows] = acc_new

    @pl.when(j < i)
    def _():
        update(False)

    @pl.when(j == i)
    def _():
        update(True)
        o_ref[...] = acc_ref[...] / l_ref[...]


def mla_prompt(q3, ckv, kpe, *, t=512):
    s = ckv.shape[0]
    t = min(t, s)
    qi, kj = _pair_tables(s // t)
    return pl.pallas_call(
        functools.partial(_mla_prompt_body, t=t),
        grid_spec=pltpu.PrefetchScalarGridSpec(
            num_scalar_prefetch=2,
            grid=(H_C, qi.shape[0]),
            in_specs=[pl.BlockSpec((None, t, KV_LORA + ROPE_C), lambda hh, n, qi, kj: (hh, qi[n], 0)),
                      pl.BlockSpec((t, KV_LORA), lambda hh, n, qi, kj: (kj[n], 0)),
                      pl.BlockSpec((t, ROPE_C), lambda hh, n, qi, kj: (kj[n], 0))],
            out_specs=pl.BlockSpec((None, t, KV_LORA), lambda hh, n, qi, kj: (hh, qi[n], 0)),
            scratch_shapes=[pltpu.VMEM((t, 1), F32), pltpu.VMEM((t, 1), F32), pltpu.VMEM((t, KV_LORA), F32)]),
        out_shape=jax.ShapeDtypeStruct((H_C, s, KV_LORA), F32),
        compiler_params=_cparams(("parallel", "arbitrary")),
        name="mla_prompt",
    )(qi, kj, q3, ckv, kpe)


CHUNK_PAGES = 16
CHUNK = CHUNK_PAGES * PAGE_SIZE
T_S = 8


def _chunk_schedule(pt_ref, b, k, n_chunks, n_seq):
    c = n_chunks - 1 - k
    last = k == n_chunks - 1
    nb = jnp.where(last, b + 1, b)
    nc = jnp.where(last, n_chunks - 1, c - 1)
    return c, nb, nc, jnp.logical_or(jnp.logical_not(last), b + 1 < n_seq)


def _even_copies(pt_ref, caches, bufs, sem, b, c, slot):
    out = []
    for p in range(CHUNK_PAGES):
        page = pt_ref[b, c * CHUNK_PAGES + p]
        for a, (src, dst) in enumerate(zip(caches, bufs)):
            out.append(pltpu.make_async_copy(src.at[page], dst.at[slot, p], sem.at[slot, a]))
    return out


def _even_sample_body(pt_ref, hs_ref, cs_ref, cst_ref, lq1, lk1, lq2, lk2, g_ref,
                      dk_hbm, dv_hbm, fk_hbm, fv_hbm, lf_hbm, od_ref, of_ref,
                      dkbuf, dvbuf, fkbuf, fvbuf, lfbuf, sem, ktb, vdb, kfb, vfb,
                      md, ld, accd, mf, lff, accf, run_ref, *, n_chunks, n_seq, past_len, lam_init):
    b = pl.program_id(0)
    caches = (dk_hbm, dv_hbm, fk_hbm, fv_hbm, lf_hbm)
    bufs = (dkbuf, dvbuf, fkbuf, fvbuf, lfbuf)
    n_d, n_f = 2 * KV_A * G_A * T_S, KV_B * G_B * T_S

    @pl.when(b == 0)
    def _():
        for cp in _even_copies(pt_ref, caches, bufs, sem, 0, n_chunks - 1, 0):
            cp.start()

    hs = hs_ref[...]
    lane = lax.broadcasted_iota(jnp.int32, (T_S, LANE), 1)
    zero = jnp.zeros((T_S, LANE), F32)
    rows = []
    for kv in range(KV_A):
        for c in range(2):
            for g in range(G_A):
                qh = hs[:, OFF_QD + (kv * G_A + g) * LANE:OFF_QD + (kv * G_A + g + 1) * LANE] * DIFF_SCALE
                qh = jnp.where((lane >= c * DQK_A) & (lane < (c + 1) * DQK_A), qh, 0.0)
                rows.append(jnp.concatenate([qh, zero] if kv == 0 else [zero, qh], axis=1))
    qd = jnp.concatenate(rows, axis=0).astype(BF16)
    rows = []
    for hh in range(H_B):
        qh = hs[:, OFF_QF + hh * LANE:OFF_QF + (hh + 1) * LANE] * FOX_SCALE
        rows.append(jnp.concatenate([qh, zero] if hh // G_B == 0 else [zero, qh], axis=1))
    qf = jnp.concatenate(rows, axis=0).astype(BF16)

    rd = lax.broadcasted_iota(jnp.int32, (n_d, 1), 0)
    head_d = (rd // (2 * G_A * T_S)) * G_A + (rd // T_S) % G_A
    slope = jnp.exp2(-(head_d + 1).astype(F32))
    t_d = rd % T_S
    qpos_d = (past_len + t_d).astype(F32)
    rf = lax.broadcasted_iota(jnp.int32, (n_f, 1), 0)
    t_f = rf % T_S
    cs = cs_ref[...]
    cq = jnp.concatenate([cs[:, hh:hh + 1] for hh in range(H_B)], axis=0)

    md[...] = jnp.full_like(md, NEG_INF)
    ld[...] = jnp.zeros_like(ld)
    accd[...] = jnp.zeros_like(accd)
    mf[...] = jnp.full_like(mf, NEG_INF)
    lff[...] = jnp.zeros_like(lff)
    accf[...] = jnp.zeros_like(accf)
    run_ref[...] = jnp.zeros_like(run_ref)

    pad = jnp.zeros((LANE - T_S, 2 * LANE), F32)
    key = lax.broadcasted_iota(jnp.int32, (1, LANE), 1)
    kd_new = jnp.concatenate([hs[:, OFF_KD:OFF_KD + 2 * LANE], pad], axis=0).astype(BF16)
    vd_new = jnp.concatenate([hs[:, OFF_VD:OFF_VD + 2 * LANE], pad], axis=0).astype(BF16)
    kf_new = jnp.concatenate([hs[:, OFF_KF:OFF_KF + 2 * LANE], pad], axis=0).astype(BF16)
    vf_new = jnp.concatenate([hs[:, OFF_VF:OFF_VF + 2 * LANE], pad], axis=0).astype(BF16)
    s = _dot_nt(qd, kd_new) - slope * (t_d - key).astype(F32)
    _online_update(jnp.where(key <= t_d, s, NEG_INF), vd_new, md, ld, accd)
    cst = cst_ref[...]
    cst_rows = jnp.broadcast_to(cst[:, None, :], (H_B, T_S, LANE)).reshape(n_f, LANE)
    s = _dot_nt(qf, kf_new) + (cq - cst_rows)
    _online_update(jnp.where(key <= t_f, s, NEG_INF), vf_new, mf, lff, accf)

    jj = lax.broadcasted_iota(jnp.int32, (PAGE_SIZE, PAGE_SIZE), 0)
    ss = lax.broadcasted_iota(jnp.int32, (PAGE_SIZE, PAGE_SIZE), 1)
    later = jnp.where(jj > ss, 1.0, 0.0).astype(BF16)
    kidx = lax.broadcasted_iota(jnp.int32, (1, CHUNK), 1)

    def step(k, carry):
        c, nb, nc, has_next = _chunk_schedule(pt_ref, b, k, n_chunks, n_seq)
        slot = k % 2

        @pl.when(has_next)
        def _():
            for cp in _even_copies(pt_ref, caches, bufs, sem, nb, nc, 1 - slot):
                cp.start()

        for cp in _even_copies(pt_ref, caches, bufs, sem, b, c, slot):
            cp.wait()

        for p in range(CHUNK_PAGES):
            lo, hi = p * PAGE_SIZE, (p + 1) * PAGE_SIZE
            ktb[:, lo:hi] = dkbuf[slot, p].astype(BF16)
            for kv in range(2):
                sl = pl.ds(kv, PAGE_SIZE, stride=2)
                vdb[lo:hi, kv * LANE:(kv + 1) * LANE] = dvbuf[slot, p, sl, :].astype(BF16)
                kfb[lo:hi, kv * LANE:(kv + 1) * LANE] = fkbuf[slot, p, sl, :].astype(BF16)
                vfb[lo:hi, kv * LANE:(kv + 1) * LANE] = fvbuf[slot, p, sl, :].astype(BF16)

        kpos = (c * CHUNK + kidx).astype(F32)
        s = _dot(qd, ktb[...]) - slope * (qpos_d - kpos)
        _online_update(s, vdb[...], md, ld, accd)

        x = lfbuf[slot].reshape(CHUNK_PAGES * H_B, PAGE_SIZE)
        within = jnp.zeros_like(x)
        for part in _split3(x):
            within = within + _dot(part, later)
        tot = jnp.sum(x, axis=-1, keepdims=True)
        run = run_ref[:, 0:1]
        pieces = [None] * CHUNK_PAGES
        for p in reversed(range(CHUNK_PAGES)):
            after = within[p * H_B:(p + 1) * H_B] + run
            pieces[p] = jnp.broadcast_to(after[:, None, :], (H_B, T_S, PAGE_SIZE)).reshape(n_f, PAGE_SIZE)
            run = run + tot[p * H_B:(p + 1) * H_B]
        run_ref[...] = jnp.broadcast_to(run, run_ref.shape)
        s = _dot_nt(qf, kfb[...]) + (cq + jnp.concatenate(pieces, axis=1))
        _online_update(s, vfb[...], mf, lff, accf)
        return carry

    lax.fori_loop(0, n_chunks, step, 0)

    lam = _diff_lambda(lq1[...], lk1[...], lq2[...], lk2[...], lam_init)
    od = accd[...] / ld[...]
    g = g_ref[...]
    for kv in range(KV_A):
        for gg in range(G_A):
            r0 = ((kv * 2 + 0) * G_A + gg) * T_S
            r1 = ((kv * 2 + 1) * G_A + gg) * T_S
            o = od[r0:r0 + T_S, kv * LANE:(kv + 1) * LANE] - lam * od[r1:r1 + T_S, kv * LANE:(kv + 1) * LANE]
            od_ref[:, (kv * G_A + gg) * LANE:(kv * G_A + gg + 1) * LANE] = _subln(o, g, lam_init)
    of = accf[...] / lff[...]
    for hh in range(H_B):
        kv = hh // G_B
        of_ref[:, hh * LANE:(hh + 1) * LANE] = of[hh * T_S:(hh + 1) * T_S, kv * LANE:(kv + 1) * LANE]


def even_sample(page_table, hs, cs, cst, lq1, lk1, lq2, lk2, g_subln, dk_t, dv2, fk2, fv2, lf_t, lam_init):
    n_seq, n_pages = page_table.shape
    assert n_pages % (2 * CHUNK_PAGES) == 0 and hs.shape[0] == n_seq * T_S
    n_chunks = n_pages // CHUNK_PAGES
    width = hs.shape[1]
    small = lambda b, pt: (0, 0)
    anyspec = pl.BlockSpec(memory_space=pl.ANY)
    page_buf = lambda: pltpu.VMEM((2, CHUNK_PAGES, 2 * LANE, PAGE_SIZE), F32)
    n_d, n_f = 2 * KV_A * G_A * T_S, KV_B * G_B * T_S
    return pl.pallas_call(
        functools.partial(_even_sample_body, n_chunks=n_chunks, n_seq=n_seq, past_len=n_pages * PAGE_SIZE, lam_init=lam_init),
        grid_spec=pltpu.PrefetchScalarGridSpec(
            num_scalar_prefetch=1,
            grid=(n_seq,),
            in_specs=[pl.BlockSpec((T_S, width), lambda b, pt: (b, 0)),
                      pl.BlockSpec((T_S, LANE), lambda b, pt: (b, 0)),
                      pl.BlockSpec((None, H_B, LANE), lambda b, pt: (b, 0, 0)),
                      pl.BlockSpec((1, DQK_A), small), pl.BlockSpec((1, DQK_A), small),
                      pl.BlockSpec((1, DQK_A), small), pl.BlockSpec((1, DQK_A), small),
                      pl.BlockSpec((1, DV_A), small),
                      anyspec, anyspec, anyspec, anyspec, anyspec],
            out_specs=[pl.BlockSpec((T_S, H_A * DV_A), lambda b, pt: (b, 0)),
                       pl.BlockSpec((T_S, H_B * D_B), lambda b, pt: (b, 0))],
            scratch_shapes=[page_buf(), page_buf(), page_buf(), page_buf(),
                            pltpu.VMEM((2, CHUNK_PAGES, H_B, PAGE_SIZE), F32),
                            pltpu.SemaphoreType.DMA((2, 5)),
                            pltpu.VMEM((2 * LANE, CHUNK), BF16), pltpu.VMEM((CHUNK, 2 * LANE), BF16),
                            pltpu.VMEM((CHUNK, 2 * LANE), BF16), pltpu.VMEM((CHUNK, 2 * LANE), BF16),
                            pltpu.VMEM((n_d, 1), F32), pltpu.VMEM((n_d, 1), F32), pltpu.VMEM((n_d, 2 * LANE), F32),
                            pltpu.VMEM((n_f, 1), F32), pltpu.VMEM((n_f, 1), F32), pltpu.VMEM((n_f, 2 * LANE), F32),
                            pltpu.VMEM((H_B, LANE), F32)]),
        out_shape=[jax.ShapeDtypeStruct((n_seq * T_S, H_A * DV_A), F32), jax.ShapeDtypeStruct((n_seq * T_S, H_B * D_B), F32)],
        compiler_params=_cparams(("arbitrary",)),
        name="even_sample",
    )(page_table, hs, cs, cst, lq1, lk1, lq2, lk2, g_subln, dk_t, dv2, fk2, fv2, lf_t)


def _mla_copies(pt_ref, caches, bufs, sem, b, c, slot):
    out = []
    for p in range(CHUNK_PAGES):
        page = pt_ref[b, c * CHUNK_PAGES + p]
        for a, (src, dst) in enumerate(zip(caches, bufs)):
            out.append(pltpu.make_async_copy(src.at[page], dst.at[slot, p], sem.at[slot, a]))
    return out


def _mla_sample_body(pt_ref, q_ref, ckvn_ref, kpen_ref, ckv_hbm, kpe_hbm, o_ref,
                     ckvbuf, kpebuf, sem, ckvb, kpeb, m_ref, l_ref, acc_ref, *, n_chunks, n_seq):
    b = pl.program_id(0)
    caches = (ckv_hbm, kpe_hbm)
    bufs = (ckvbuf, kpebuf)
    n_r = H_C * T_S

    @pl.when(b == 0)
    def _():
        for cp in _mla_copies(pt_ref, caches, bufs, sem, 0, n_chunks - 1, 0):
            cp.start()

    q = q_ref[...].reshape(n_r, KV_LORA + ROPE_C) * MLA_SCALE
    ql = q[:, :KV_LORA].astype(BF16)
    qp = q[:, KV_LORA:].astype(BF16)
    t_r = lax.broadcasted_iota(jnp.int32, (n_r, 1), 0) % T_S
    key = lax.broadcasted_iota(jnp.int32, (1, LANE), 1)

    m_ref[...] = jnp.full_like(m_ref, NEG_INF)
    l_ref[...] = jnp.zeros_like(l_ref)
    acc_ref[...] = jnp.zeros_like(acc_ref)

    ckv_new = jnp.concatenate([ckvn_ref[...], jnp.zeros((LANE - T_S, KV_LORA), F32)], axis=0).astype(BF16)
    kpe_new = jnp.concatenate([kpen_ref[...], jnp.zeros((LANE - T_S, ROPE_C), F32)], axis=0).astype(BF16)
    s = _dot_nt(ql, ckv_new) + _dot_nt(qp, kpe_new)
    _online_update(jnp.where(key <= t_r, s, NEG_INF), ckv_new, m_ref, l_ref, acc_ref)

    def step(k, carry):
        c, nb, nc, has_next = _chunk_schedule(pt_ref, b, k, n_chunks, n_seq)
        slot = k % 2

        @pl.when(has_next)
        def _():
            for cp in _mla_copies(pt_ref, caches, bufs, sem, nb, nc, 1 - slot):
                cp.start()

        for cp in _mla_copies(pt_ref, caches, bufs, sem, b, c, slot):
            cp.wait()

        for p in range(CHUNK_PAGES):
            lo, hi = p * PAGE_SIZE, (p + 1) * PAGE_SIZE
            ckvb[lo:hi, :] = ckvbuf[slot, p].astype(BF16)
            kpeb[:, lo:hi] = kpebuf[slot, p].astype(BF16)

        kv = ckvb[...]
        s = _dot_nt(ql, kv) + _dot(qp, kpeb[...])
        _online_update(s, kv, m_ref, l_ref, acc_ref)
        return carry

    lax.fori_loop(0, n_chunks, step, 0)
    o_ref[...] = (acc_ref[...] / l_ref[...]).reshape(H_C, T_S, KV_LORA)


def mla_sample(page_table, q3, ckv_new, kpe_new, ckv_pages, kpe_t):
    n_seq, n_pages = page_table.shape
    assert n_pages % (2 * CHUNK_PAGES) == 0
    n_chunks = n_pages // CHUNK_PAGES
    anyspec = pl.BlockSpec(memory_space=pl.ANY)
    n_r = H_C * T_S
    return pl.pallas_call(
        functools.partial(_mla_sample_body, n_chunks=n_chunks, n_seq=n_seq),
        grid_spec=pltpu.PrefetchScalarGridSpec(
            num_scalar_prefetch=1,
            grid=(n_seq,),
            in_specs=[pl.BlockSpec((H_C, T_S, KV_LORA + ROPE_C), lambda b, pt: (0, b, 0)),
                      pl.BlockSpec((T_S, KV_LORA), lambda b, pt: (b, 0)),
                      pl.BlockSpec((T_S, ROPE_C), lambda b, pt: (b, 0)),
                      anyspec, anyspec],
            out_specs=pl.BlockSpec((H_C, T_S, KV_LORA), lambda b, pt: (0, b, 0)),
            scratch_shapes=[pltpu.VMEM((2, CHUNK_PAGES, PAGE_SIZE, KV_LORA), F32),
                            pltpu.VMEM((2, CHUNK_PAGES, ROPE_C, PAGE_SIZE), F32),
                            pltpu.SemaphoreType.DMA((2, 2)),
                            pltpu.VMEM((CHUNK, KV_LORA), BF16), pltpu.VMEM((ROPE_C, CHUNK), BF16),
                            pltpu.VMEM((n_r, 1), F32), pltpu.VMEM((n_r, 1), F32), pltpu.VMEM((n_r, KV_LORA), F32)]),
        out_shape=jax.ShapeDtypeStruct((H_C, n_seq * T_S, KV_LORA), F32),
        compiler_params=_cparams(("arbitrary",)),
        name="mla_sample",
    )(page_table, q3, ckv_new, kpe_new, ckv_pages, kpe_t)


def _rope(x, cos2, sin2):
    half = x.shape[-1] // 2
    swapped = jnp.concatenate([x[:, half:], x[:, :half]], axis=1)
    return x * cos2 + swapped * sin2


def _rms(v, g, eps):
    return v * lax.rsqrt(jnp.mean(v * v, axis=-1, keepdims=True) + eps) * g


def _mla_cq_body(x_ref, w_ref, g_ref, o_ref):
    cq = _dot(x_ref[...].astype(BF16), w_ref[...].astype(BF16))
    o_ref[...] = _rms(cq, g_ref[...], RMS_EPS)


def mla_cq(x, w_dq, g_qn, *, tm=512):
    m, k = x.shape
    tm = min(tm, m)
    return pl.pallas_call(
        _mla_cq_body,
        grid=(m // tm,),
        in_specs=[pl.BlockSpec((tm, k), lambda i: (i, 0)), pl.BlockSpec((k, Q_LORA), lambda i: (0, 0)),
                  pl.BlockSpec((1, Q_LORA), lambda i: (0, 0))],
        out_specs=pl.BlockSpec((tm, Q_LORA), lambda i: (i, 0)),
        out_shape=jax.ShapeDtypeStruct((m, Q_LORA), F32),
        compiler_params=_cparams(("parallel",)),
        name="mla_cq",
    )(x, w_dq, g_qn)


def _mla_q_body(cq_ref, wuq_ref, wuk_ref, cos_ref, sin_ref, o_ref):
    q = _dot_nt(cq_ref[...].astype(BF16), wuq_ref[...].astype(BF16))
    q_lat = _dot_nt(q[:, :NOPE_C].astype(BF16), wuk_ref[...].astype(BF16))
    o_ref[:, :KV_LORA] = q_lat
    o_ref[:, KV_LORA:] = _rope(q[:, NOPE_C:], cos_ref[...], sin_ref[...])


def mla_q(cq, wuq_t, w_ukv2, cos2, sin2, *, tm=512):
    m = cq.shape[0]
    tm = min(tm, m)
    return pl.pallas_call(
        _mla_q_body,
        grid=(H_C, m // tm),
        in_specs=[pl.BlockSpec((tm, Q_LORA), lambda h, i: (i, 0)),
                  pl.BlockSpec((None, NOPE_C + ROPE_C, Q_LORA), lambda h, i: (h, 0, 0)),
                  pl.BlockSpec((KV_LORA, NOPE_C), lambda h, i: (0, 2 * h)),
                  pl.BlockSpec((tm, ROPE_C), lambda h, i: (i, 0)), pl.BlockSpec((tm, ROPE_C), lambda h, i: (i, 0))],
        out_specs=pl.BlockSpec((None, tm, KV_LORA + ROPE_C), lambda h, i: (h, i, 0)),
        out_shape=jax.ShapeDtypeStruct((H_C, m, KV_LORA + ROPE_C), F32),
        compiler_params=_cparams(("parallel", "parallel")),
        name="mla_q",
    )(cq, wuq_t, w_ukv2, cos2, sin2)


def _mla_kv_body(x_ref, w_ref, g_ref, cos_ref, sin_ref, ckv_ref, kpe_ref):
    kv = _dot_nt(x_ref[...].astype(BF16), w_ref[...].astype(BF16))
    ckv_ref[...] = _rms(kv[:, :KV_LORA], g_ref[...], RMS_EPS)
    kpe_ref[...] = _rope(kv[:, KV_LORA:], cos_ref[...], sin_ref[...])


def mla_kv(x, wdkv_t, g_kvn, cos2, sin2, *, tm=512):
    m, k = x.shape
    tm = min(tm, m)
    return pl.pallas_call(
        _mla_kv_body,
        grid=(m // tm,),
        in_specs=[pl.BlockSpec((tm, k), lambda i: (i, 0)), pl.BlockSpec((KV_LORA + ROPE_C, k), lambda i: (0, 0)),
                  pl.BlockSpec((1, KV_LORA), lambda i: (0, 0)),
                  pl.BlockSpec((tm, ROPE_C), lambda i: (i, 0)), pl.BlockSpec((tm, ROPE_C), lambda i: (i, 0))],
        out_specs=[pl.BlockSpec((tm, KV_LORA), lambda i: (i, 0)), pl.BlockSpec((tm, ROPE_C), lambda i: (i, 0))],
        out_shape=[jax.ShapeDtypeStruct((m, KV_LORA), F32), jax.ShapeDtypeStruct((m, ROPE_C), F32)],
        compiler_params=_cparams(("parallel",)),
        name="mla_kv",
    )(x, wdkv_t, g_kvn, cos2, sin2)


def _mla_ov_body(o_ref, wuv_ref, out_ref):
    out_ref[...] = _dot(o_ref[...].astype(BF16), wuv_ref[...].astype(BF16))


def mla_ov(o3, w_ukv2, *, tm=512):
    m = o3.shape[1]
    tm = min(tm, m)
    return pl.pallas_call(
        _mla_ov_body,
        grid=(H_C, m // tm),
        in_specs=[pl.BlockSpec((None, tm, KV_LORA), lambda h, i: (h, i, 0)),
                  pl.BlockSpec((KV_LORA, V_C), lambda h, i: (0, 2 * h + 1))],
        out_specs=pl.BlockSpec((tm, V_C), lambda h, i: (i, h)),
        out_shape=jax.ShapeDtypeStruct((m, H_C * V_C), F32),
        compiler_params=_cparams(("parallel", "parallel")),
        name="mla_ov",
    )(o3, w_ukv2)


def _router_body(x_ref, w_ref, b_ref, gates_ref):
    x = x_ref[...]
    w = w_ref[...]
    xh = x.astype(BF16)
    xl = (x - xh.astype(F32)).astype(BF16)
    wh = w.astype(BF16)
    wl = (w - wh.astype(F32)).astype(BF16)
    logit = _dot(xh, wh) + _dot(xh, wl) + _dot(xl, wh) + b_ref[...]
    lane = lax.broadcasted_iota(jnp.int32, logit.shape, 1)
    big = jnp.int32(2 ** 30)

    def first_max(mask):
        v = jnp.max(jnp.where(mask, logit, NEG_INF), axis=-1, keepdims=True)
        idx = jnp.min(jnp.where(mask & (logit == v), lane, big), axis=-1, keepdims=True)
        return v, idx

    is_group = (lane >= N_EXPERTS) & (lane < N_EXPERTS + N_GROUPS)
    gmax, gidx = first_max(is_group)
    w_grp = 1.0 / jnp.sum(jnp.where(is_group, jnp.exp(logit - gmax), 0.0), axis=-1, keepdims=True)
    g_sel = gidx - N_EXPERTS
    in_grp = (lane >= g_sel * EXP_PER_GROUP) & (lane < (g_sel + 1) * EXP_PER_GROUP)
    v1, i1 = first_max(in_grp)
    v2, i2 = first_max(in_grp & (lane != i1))
    e2 = jnp.exp(v2 - v1)
    w1 = w_grp / (1.0 + e2)
    w2 = w_grp * e2 / (1.0 + e2)
    route = jnp.where(lane == 0, i1.astype(F32), 0.0) + jnp.where(lane == 1, i2.astype(F32), 0.0)
    gates_ref[...] = route + jnp.where(lane == 2, w1, 0.0) + jnp.where(lane == 3, w2, 0.0)


def router(x, w_pad, b_pad, *, tm=512):
    m, k = x.shape
    tm = min(tm, m)
    return pl.pallas_call(
        _router_body,
        grid=(m // tm,),
        in_specs=[pl.BlockSpec((tm, k), lambda i: (i, 0)), pl.BlockSpec((k, LANE), lambda i: (0, 0)),
                  pl.BlockSpec((1, LANE), lambda i: (0, 0))],
        out_specs=pl.BlockSpec((tm, LANE), lambda i: (i, 0)),
        out_shape=jax.ShapeDtypeStruct((m, LANE), F32),
        compiler_params=_cparams(("parallel",)),
        name="router",
    )(x, w_pad, b_pad)


MOE_TILE = 256
ROW_UNROLL = 8


def _row_gather(idx_smem, islot, src_hbm, dst, dslot, sem, n_rows, wait):
    def body(r, carry):
        row = idx_smem[islot, r]
        cp = pltpu.make_async_copy(src_hbm.at[pl.ds(row, 1), :], dst.at[dslot, pl.ds(r, 1), :], sem.at[dslot])
        cp.wait() if wait else cp.start()
        return carry
    lax.fori_loop(0, n_rows, body, 0, unroll=ROW_UNROLL)


def _gather_pipeline(t, n_act, idx_hbm, idx_smem, isem, src_hbm, dst, gsem, n_rows):
    idx_copy = lambda tile: pltpu.make_async_copy(idx_hbm.at[tile], idx_smem.at[tile % 3], isem.at[tile % 3])

    @pl.when(t == 0)
    def _():
        idx_copy(0).start()
        idx_copy(0).wait()
        _row_gather(idx_smem, 0, src_hbm, dst, 0, gsem, n_rows, wait=False)

        @pl.when(n_act > 1)
        def _():
            idx_copy(1).start()

    @pl.when(t + 1 < n_act)
    def _():
        idx_copy(t + 1).wait()
        _row_gather(idx_smem, (t + 1) % 3, src_hbm, dst, (t + 1) % 2, gsem, n_rows, wait=False)

        @pl.when(t + 2 < n_act)
        def _():
            idx_copy(t + 2).start()

    @pl.when(t < n_act)
    def _():
        _row_gather(idx_smem, t % 3, src_hbm, dst, t % 2, gsem, n_rows, wait=True)


def _moe_ffn_body(te_ref, na_ref, tok_hbm, x_hbm, gate_ref, wg_ref, wu_ref, wd_ref, o_ref,
                  xbuf, tok_smem, gsem, isem, wgb, wub, wdb):
    t = pl.program_id(0)
    n_act = na_ref[0]
    _gather_pipeline(t, n_act, tok_hbm, tok_smem, isem, x_hbm, xbuf, gsem, MOE_TILE)

    @pl.when(t < n_act)
    def _():
        @pl.when((t == 0) | (te_ref[t] != te_ref[jnp.maximum(t - 1, 0)]))
        def _():
            wgb[...] = wg_ref[...].astype(BF16)
            wub[...] = wu_ref[...].astype(BF16)
            wdb[...] = wd_ref[...].astype(BF16)

        xb = xbuf[t % 2].astype(BF16)
        a = _dot(xb, wgb[...])
        u = _dot(xb, wub[...])
        hmid = a * (1.0 / (1.0 + jnp.exp(-a))) * u * gate_ref[...]
        o_ref[...] = _dot(hmid.astype(BF16), wdb[...])

    @pl.when(t >= n_act)
    def _():
        o_ref[...] = jnp.zeros_like(o_ref)


def moe_ffn(tile_expert, n_active, tok_tiles, x, gate_col, w_gate, w_up, w_down, layer):
    n_tiles = tok_tiles.shape[0]
    d = x.shape[1]
    wspec = lambda shape: pl.BlockSpec((None, None) + shape, lambda t, te, na: (layer, te[t], 0, 0))
    return pl.pallas_call(
        _moe_ffn_body,
        grid_spec=pltpu.PrefetchScalarGridSpec(
            num_scalar_prefetch=2,
            grid=(n_tiles,),
            in_specs=[pl.BlockSpec(memory_space=pl.ANY), pl.BlockSpec(memory_space=pl.ANY),
                      pl.BlockSpec((MOE_TILE, 1), lambda t, te, na: (t, 0)),
                      wspec((d, D_EXPERT)), wspec((d, D_EXPERT)), wspec((D_EXPERT, d))],
            out_specs=pl.BlockSpec((MOE_TILE, d), lambda t, te, na: (t, 0)),
            scratch_shapes=[pltpu.VMEM((2, MOE_TILE, d), F32), pltpu.SMEM((3, MOE_TILE), jnp.int32),
                            pltpu.SemaphoreType.DMA((2,)), pltpu.SemaphoreType.DMA((3,)),
                            pltpu.VMEM((d, D_EXPERT), BF16), pltpu.VMEM((d, D_EXPERT), BF16),
                            pltpu.VMEM((D_EXPERT, d), BF16)]),
        out_shape=jax.ShapeDtypeStruct((n_tiles * MOE_TILE, d), F32),
        compiler_params=_cparams(("arbitrary",)),
        name="moe_ffn",
    )(tile_expert, n_active, tok_tiles, x, gate_col, w_gate, w_up, w_down)


def _moe_combine_body(pos_hbm, y_hbm, x_ref, g_ref, b_ref, o_ref, ybuf, pos_smem, gsem, isem, *, tm):
    t = pl.program_id(0)
    _gather_pipeline(t, pl.num_programs(0), pos_hbm, pos_smem, isem, y_hbm, ybuf, gsem, 2 * tm)
    y = ybuf[t % 2, :tm] + ybuf[t % 2, tm:]
    o_ref[...] = _layer_norm(ALPHA * x_ref[...] + y, g_ref[...], b_ref[...])


def moe_combine_ln(pos_tiles, y_sorted, x, g, b, *, tm):
    m, d = x.shape
    row = lambda i: (i, 0)
    fixed = lambda i: (0, 0)
    return pl.pallas_call(
        functools.partial(_moe_combine_body, tm=tm),
        grid=(m // tm,),
        in_specs=[pl.BlockSpec(memory_space=pl.ANY), pl.BlockSpec(memory_space=pl.ANY),
                  pl.BlockSpec((tm, d), row), pl.BlockSpec((1, d), fixed), pl.BlockSpec((1, d), fixed)],
        out_specs=pl.BlockSpec((tm, d), row),
        out_shape=jax.ShapeDtypeStruct((m, d), F32),
        scratch_shapes=[pltpu.VMEM((2, 2 * tm, d), F32), pltpu.SMEM((3, 2 * tm), jnp.int32),
                        pltpu.SemaphoreType.DMA((2,)), pltpu.SemaphoreType.DMA((3,))],
        compiler_params=_cparams(("arbitrary",)),
        name="moe_combine_ln",
    )(pos_tiles, y_sorted, x, g, b)


def _routing_tables(route, tm_tok):
    n = route.shape[0]
    n_asg = 2 * n
    n_tiles = (n_asg + N_EXPERTS * MOE_TILE) // MOE_TILE
    ef = route[:, 0:2].astype(jnp.int32).reshape(n_asg)
    wf = route[:, 2:4].reshape(n_asg)
    order = jnp.argsort(ef, stable=True).astype(jnp.int32)
    counts = jnp.sum((ef[:, None] == jnp.arange(N_EXPERTS, dtype=jnp.int32)[None, :]).astype(jnp.int32), axis=0)
    padded = ((counts + MOE_TILE - 1) // MOE_TILE) * MOE_TILE
    seg_end_pad = jnp.cumsum(padded)
    seg_start_pad = seg_end_pad - padded
    seg_start = jnp.cumsum(counts) - counts
    n_active = (seg_end_pad[-1] // MOE_TILE).astype(jnp.int32)
    tile_start = jnp.arange(n_tiles, dtype=jnp.int32) * MOE_TILE
    tile_expert = jnp.minimum(jnp.searchsorted(seg_end_pad, tile_start, side="right"), N_EXPERTS - 1).astype(jnp.int32)
    tile_expert = jnp.where(jnp.arange(n_tiles) < n_active, tile_expert, tile_expert[jnp.maximum(n_active - 1, 0)])
    p = jnp.arange(n_tiles * MOE_TILE, dtype=jnp.int32)
    e_p = jnp.repeat(tile_expert, MOE_TILE)
    local = p - seg_start_pad[e_p]
    valid = (local < counts[e_p]) & (p < seg_end_pad[-1])
    rank = jnp.clip(seg_start[e_p] + local, 0, n_asg - 1)
    asg = order[rank]
    row_token = jnp.where(valid, asg // 2, 0).astype(jnp.int32)
    row_gate = jnp.where(valid, wf[asg], 0.0)
    e_sorted = ef[order]
    ppos = seg_start_pad[e_sorted] + (jnp.arange(n_asg, dtype=jnp.int32) - seg_start[e_sorted])
    pos_a = jnp.zeros((n_asg,), jnp.int32).at[order].set(ppos.astype(jnp.int32), unique_indices=True)
    pos_tiles = jnp.transpose(pos_a.reshape(n // tm_tok, tm_tok, 2), (0, 2, 1)).reshape(n // tm_tok, 2 * tm_tok)
    return (tile_expert, n_active.reshape(1), row_token.reshape(n_tiles, MOE_TILE),
            row_gate.reshape(n_tiles * MOE_TILE, 1), pos_tiles)


def _rope_tables(pos):
    half = ROPE_C // 2
    freqs = ROPE_THETA ** (-jnp.arange(half, dtype=F32) / half)
    ang = pos.astype(F32)[:, None] * freqs
    cos, sin = jnp.cos(ang), jnp.sin(ang)
    return jnp.concatenate([cos, cos], axis=1), jnp.concatenate([-sin, sin], axis=1)


def _lambda_init(layer_idx):
    return 0.8 - 0.6 * math.exp(-0.3 * layer_idx)


def _row(v):
    return v.reshape(1, -1)


def _moe_ln_block(x, l, g, b, w_router_group, b_router_group, w_router_expert, b_router_expert, w_gate, w_up, w_down):
    pad = jnp.zeros((D_MODEL, LANE - N_EXPERTS - N_GROUPS), F32)
    w_pad = jnp.concatenate([w_router_expert[l], w_router_group[l], pad], axis=1)
    b_pad = jnp.concatenate([b_router_expert[l], b_router_group[l], jnp.zeros((LANE - N_EXPERTS - N_GROUPS,), F32)])[None]
    tm_tok = math.gcd(x.shape[0], MOE_TILE)
    route = router(x, w_pad, b_pad, tm=tm_tok)
    tile_expert, n_active, tok_tiles, gate_col, pos_tiles = _routing_tables(route, tm_tok)
    y_sorted = moe_ffn(tile_expert, n_active, tok_tiles, x, gate_col, w_gate, w_up, w_down, l)
    return moe_combine_ln(pos_tiles, y_sorted, x, g, b, tm=tm_tok)


def kernel(x_prompt, x_sample, cache_diff_k, cache_diff_v, cache_fox_k, cache_fox_v, cache_fox_logf, cache_mla_ckv, cache_mla_kpe, page_table, w_in_even, b_forget, lambda_q1, lambda_k1, lambda_q2, lambda_k2, g_subln, w_out_even, w_dq, g_q_norm, w_uq, w_dkv, g_kv_norm, w_ukv, w_o_mla, ln_attn_g, ln_attn_b, ln_ffn_g, ln_ffn_b, w_router_group, b_router_group, w_router_expert, b_router_expert, w_gate, w_up, w_down):
    n_b, s_len, d = x_prompt.shape
    n_seq, t_new, _ = x_sample.shape
    assert n_b == 1 and t_new == T_S
    n_pool = cache_diff_k.shape[1]
    past = page_table.shape[1] * PAGE_SIZE
    xp = x_prompt.reshape(s_len, d)
    xs = x_sample.reshape(n_seq * T_S, d)
    slopes = 2.0 ** (-8.0 * jnp.arange(1, H_A + 1, dtype=F32) / H_A)
    moe_w = (w_router_group, b_router_group, w_router_expert, b_router_expert, w_gate, w_up, w_down)
    outs_p, outs_s = {}, {}

    for l in range(DEPTH):
        i = l // 2
        if l % 2 == 0:
            lam_init = _lambda_init(l)
            w_t = jnp.transpose(w_in_even[i])
            wft = w_t[OFF_FL:OFF_FL + H_B]
            wf_pad = jnp.concatenate([jnp.transpose(wft), jnp.zeros((d, LANE - H_B), F32)], axis=1)
            b_row = jnp.concatenate([b_forget[i], jnp.zeros((LANE - H_B,), F32)])[None]
            b_col = b_forget[i][:, None]
            lam_args = (_row(lambda_q1[i]), _row(lambda_k1[i]), _row(lambda_q2[i]), _row(lambda_k2[i]), _row(g_subln[i]))
            hp = matmul(xp, w_t, nt=True, n_out=OFF_FL, name="even_proj_p")
            lf_p, cum_p, cumt_p = forget_gates(xp, wf_pad, wft, b_row, b_col, seg=s_len)
            od = diff_prompt(hp, slopes, *lam_args, lam_init)
            of = fox_prompt(hp, cum_p, cumt_p.reshape(H_B, 1, s_len))
            xp = matmul_ln(jnp.concatenate([od, of], axis=1), w_out_even[i], xp,
                           _row(ln_attn_g[l]), _row(ln_attn_b[l]), name="even_out_p")
            outs_p[l] = (hp[:, OFF_KD:OFF_VD].reshape(1, s_len, KV_A, 2, DQK_A), hp[:, OFF_VD:OFF_QF].reshape(1, s_len, KV_A, DV_A),
                         hp[:, OFF_KF:OFF_VF].reshape(1, s_len, KV_B, D_B), hp[:, OFF_VF:OFF_FL].reshape(1, s_len, KV_B, D_B),
                         lf_p[:, :H_B].reshape(1, s_len, H_B))
            hs = matmul(xs, w_t, nt=True, n_out=OFF_FL, name="even_proj_s")
            lf_s, cum_s, cumt_s = forget_gates(xs, wf_pad, wft, b_row, b_col, seg=T_S)
            cst = jnp.transpose(cumt_s.reshape(H_B, n_seq, T_S), (1, 0, 2))
            cst = jnp.concatenate([cst, jnp.zeros((n_seq, H_B, LANE - T_S), F32)], axis=2)
            dk_t = jnp.transpose(cache_diff_k[i], (0, 2, 3, 4, 1)).reshape(n_pool, KV_A * 2 * DQK_A, PAGE_SIZE)
            dv2 = cache_diff_v[i].reshape(n_pool, PAGE_SIZE * KV_A, DV_A)
            fk2 = cache_fox_k[i].reshape(n_pool, PAGE_SIZE * KV_B, D_B)
            fv2 = cache_fox_v[i].reshape(n_pool, PAGE_SIZE * KV_B, D_B)
            lf_t = jnp.transpose(cache_fox_logf[i], (0, 2, 1))
            od, of = even_sample(page_table, hs, cum_s, cst, *lam_args, dk_t, dv2, fk2, fv2, lf_t, lam_init)
            xs = matmul_ln(jnp.concatenate([od, of], axis=1), w_out_even[i], xs,
                           _row(ln_attn_g[l]), _row(ln_attn_b[l]), name="even_out_s")
            outs_s[l] = (hs[:, OFF_KD:OFF_VD].reshape(n_seq, T_S, KV_A, 2, DQK_A), hs[:, OFF_VD:OFF_QF].reshape(n_seq, T_S, KV_A, DV_A),
                         hs[:, OFF_KF:OFF_VF].reshape(n_seq, T_S, KV_B, D_B), hs[:, OFF_VF:OFF_FL].reshape(n_seq, T_S, KV_B, D_B),
                         lf_s[:, :H_B].reshape(n_seq, T_S, H_B))
        else:
            wuq_t = jnp.transpose(w_uq[i], (1, 2, 0))
            wdkv_t = jnp.transpose(w_dkv[i])
            w_ukv2 = w_ukv[i].reshape(KV_LORA, H_C * (NOPE_C + V_C))
            kpe_t = jnp.transpose(cache_mla_kpe[i], (0, 2, 1))
            for grp in ("p", "s"):
                x = xp if grp == "p" else xs
                pos = jnp.arange(s_len, dtype=jnp.int32) if grp == "p" else past + jnp.tile(jnp.arange(T_S, dtype=jnp.int32), n_seq)
                cos2, sin2 = _rope_tables(pos)
                cq = mla_cq(x, w_dq[i], _row(g_q_norm[i]))
                q3 = mla_q(cq, wuq_t, w_ukv2, cos2, sin2)
                ckv, kpe = mla_kv(x, wdkv_t, _row(g_kv_norm[i]), cos2, sin2)
                if grp == "p":
                    o3 = mla_prompt(q3, ckv, kpe)
                else:
                    o3 = mla_sample(page_table, q3, ckv, kpe, cache_mla_ckv[i], kpe_t)
                ov = mla_ov(o3, w_ukv2)
                x = matmul_ln(ov, w_o_mla[i], x, _row(ln_attn_g[l]), _row(ln_attn_b[l]), name="mla_out_" + grp)
                if grp == "p":
                    xp = x
                    outs_p[l] = (ckv.reshape(1, s_len, KV_LORA), kpe.reshape(1, s_len, ROPE_C))
                else:
                    xs = x
                    outs_s[l] = (ckv.reshape(n_seq, T_S, KV_LORA), kpe.reshape(n_seq, T_S, ROPE_C))
        x_all = _moe_ln_block(jnp.concatenate([xp, xs], axis=0), l, _row(ln_ffn_g[l]), _row(ln_ffn_b[l]), *moe_w)
        xp, xs = x_all[:s_len], x_all[s_len:]

    stack = lambda outs, k, ls: jnp.stack([outs[l][k] for l in ls])
    even, odd = range(0, DEPTH, 2), range(1, DEPTH, 2)
    return (xp.reshape(1, s_len, d), xs.reshape(n_seq, T_S, d),
            *[stack(outs_p, k, even) for k in range(5)], *[stack(outs_p, k, odd) for k in range(2)],
            *[stack(outs_s, k, even) for k in range(5)], *[stack(outs_s, k, odd) for k in range(2)])
```

```python
import functools
import math

import numpy as np
import jax
import jax.numpy as jnp
from jax import lax
from jax.experimental import pallas as pl
from jax.experimental.pallas import tpu as pltpu

F32 = jnp.float32
BF16 = jnp.bfloat16

D_MODEL = 2048
PAGE_SIZE = 128
Q_BLOCK = 128
H_A, KV_A, DQK_A = 8, 2, 64
G_A = H_A // KV_A
DV_A = 2 * DQK_A
H_B, KV_B, D_B = 8, 2, 128
G_B = H_B // KV_B
H_C, NOPE_C, ROPE_C, V_C = 16, 128, 64, 128
Q_LORA, KV_LORA = 512, 256
ROPE_THETA = 10000.0
N_GROUPS, EXP_PER_GROUP = 4, 8
N_EXPERTS = N_GROUPS * EXP_PER_GROUP
D_EXPERT = 512
DEPTH = 2
ALPHA = (2 * DEPTH) ** 0.25
DIFF_SCALE = DQK_A ** -0.5
FOX_SCALE = D_B ** -0.5
MLA_SCALE = (NOPE_C + ROPE_C) ** -0.5
LN_EPS = 1e-5
RMS_EPS = 1e-6

OFF_QD = 0
OFF_KD = H_A * 2 * DQK_A
OFF_VD = OFF_KD + KV_A * 2 * DQK_A
OFF_QF = OFF_VD + KV_A * DV_A
OFF_KF = OFF_QF + H_B * D_B
OFF_VF = OFF_KF + KV_B * D_B
OFF_FL = OFF_VF + KV_B * D_B
LANE = 128
VMEM_LIMIT = 56 * 2 ** 20

NT_DIMS = (((1,), (1,)), ((), ()))
NEG_INF = float("-inf")


def _cparams(sem):
    return pltpu.CompilerParams(dimension_semantics=sem, vmem_limit_bytes=VMEM_LIMIT)


def _dot(a, b):
    return jnp.dot(a, b, preferred_element_type=F32)


def _dot_nt(a, b):
    return lax.dot_general(a, b, NT_DIMS, preferred_element_type=F32)


def _split3(v):
    hi = v.astype(BF16)
    r1 = v - hi.astype(F32)
    mid = r1.astype(BF16)
    lo = (r1 - mid.astype(F32)).astype(BF16)
    return hi, mid, lo


def _mm_body(x_ref, w_ref, s_ref, o_ref, ob_ref, *, nt):
    x = x_ref[...].astype(BF16)
    w = w_ref[...].astype(BF16)
    acc = _dot_nt(x, w) if nt else _dot(x, w)
    o_ref[...] = acc
    ob_ref[...] = (acc * s_ref[...]).astype(BF16)


def matmul_scaled(x, w, col_scale, *, nt=False, n_out=None, tm=512, tn=512, name="matmul"):
    m, k = x.shape
    n = n_out if n_out is not None else (w.shape[0] if nt else w.shape[1])
    tm, tn = min(tm, m), min(tn, n)
    assert m % tm == 0 and n % tn == 0
    w_spec = pl.BlockSpec((tn, k), lambda i, j: (j, 0)) if nt else pl.BlockSpec((k, tn), lambda i, j: (0, j))
    out_spec = pl.BlockSpec((tm, tn), lambda i, j: (i, j))
    return pl.pallas_call(
        functools.partial(_mm_body, nt=nt),
        grid=(m // tm, n // tn),
        in_specs=[pl.BlockSpec((tm, k), lambda i, j: (i, 0)), w_spec, pl.BlockSpec((1, tn), lambda i, j: (0, j))],
        out_specs=[out_spec, out_spec],
        out_shape=[jax.ShapeDtypeStruct((m, n), F32), jax.ShapeDtypeStruct((m, n), BF16)],
        compiler_params=_cparams(("parallel", "parallel")),
        name=name,
    )(x, w, col_scale)


def _layer_norm(v, g, b):
    mu = jnp.mean(v, axis=-1, keepdims=True)
    d = v - mu
    var = jnp.mean(d * d, axis=-1, keepdims=True)
    return d * lax.rsqrt(var + LN_EPS) * g + b


def _mm_ln_body(x_ref, w_ref, r_ref, g_ref, b_ref, o_ref, acc_ref):
    k = pl.program_id(1)

    @pl.when(k == 0)
    def _():
        acc_ref[...] = jnp.zeros_like(acc_ref)

    acc_ref[...] += _dot(x_ref[...].astype(BF16), w_ref[...].astype(BF16))

    @pl.when(k == pl.num_programs(1) - 1)
    def _():
        o_ref[...] = _layer_norm(ALPHA * r_ref[...] + acc_ref[...], g_ref[...], b_ref[...])


def matmul_ln(x, w, resid, g, b, *, tm=512, tk=512, name="proj_ln"):
    m, kdim = x.shape
    n = w.shape[1]
    tm = min(tm, m)
    return pl.pallas_call(
        _mm_ln_body,
        grid=(m // tm, kdim // tk),
        in_specs=[pl.BlockSpec((tm, tk), lambda i, k: (i, k)), pl.BlockSpec((tk, n), lambda i, k: (k, 0)),
                  pl.BlockSpec((tm, n), lambda i, k: (i, 0)), pl.BlockSpec((1, n), lambda i, k: (0, 0)),
                  pl.BlockSpec((1, n), lambda i, k: (0, 0))],
        out_specs=pl.BlockSpec((tm, n), lambda i, k: (i, 0)),
        out_shape=jax.ShapeDtypeStruct((m, n), F32),
        scratch_shapes=[pltpu.VMEM((tm, n), F32)],
        compiler_params=_cparams(("parallel", "arbitrary")),
        name=name,
    )(x, w, resid, g, b)


def _logsig(z):
    return jnp.minimum(z, 0.0) - jnp.log1p(jnp.exp(-jnp.abs(z)))


def _forget_body(x_ref, wf_ref, wft_ref, b_ref, bt_ref, lf_ref, cum_ref, cumt_ref, carry_ref, carryt_ref, *, tm, seg):
    i = pl.program_id(0)

    @pl.when((i * tm) % seg == 0)
    def _():
        carry_ref[...] = jnp.zeros_like(carry_ref)
        carryt_ref[...] = jnp.zeros_like(carryt_ref)

    xb = x_ref[...].astype(BF16)
    lf = _logsig(_dot(xb, wf_ref[...].astype(BF16)) + b_ref[...])
    lft = _logsig(_dot_nt(wft_ref[...].astype(BF16), xb) + bt_ref[...])
    lf_ref[...] = lf
    r = lax.broadcasted_iota(jnp.int32, (tm, tm), 0)
    c = lax.broadcasted_iota(jnp.int32, (tm, tm), 1)
    same = (r // seg) == (c // seg) if seg < tm else (r >= 0)
    lower = jnp.where((c <= r) & same, 1.0, 0.0).astype(BF16)
    upper = jnp.where((r <= c) & same, 1.0, 0.0).astype(BF16)
    cum = carry_ref[0:1, :]
    for part in _split3(lf):
        cum = cum + _dot(lower, part)
    cumt = carryt_ref[:, 0:1]
    for part in _split3(lft):
        cumt = cumt + _dot(part, upper)
    cum_ref[...] = cum
    cumt_ref[...] = cumt
    carry_ref[...] = jnp.broadcast_to(cum[tm - 1:tm, :], carry_ref.shape)
    carryt_ref[...] = jnp.broadcast_to(cumt[:, tm - 1:tm], carryt_ref.shape)


def forget_gates(x, wf_pad, wft, b_row, b_col, *, seg, tm=256):
    m, k = x.shape
    tm = min(tm, m)
    assert m % tm == 0 and (seg % tm == 0 or tm % seg == 0)
    fixed = lambda i: (0, 0)
    return pl.pallas_call(
        functools.partial(_forget_body, tm=tm, seg=seg),
        grid=(m // tm,),
        in_specs=[pl.BlockSpec((tm, k), lambda i: (i, 0)), pl.BlockSpec((k, LANE), fixed), pl.BlockSpec((H_B, k), fixed),
                  pl.BlockSpec((1, LANE), fixed), pl.BlockSpec((H_B, 1), fixed)],
        out_specs=[pl.BlockSpec((tm, LANE), lambda i: (i, 0)), pl.BlockSpec((tm, LANE), lambda i: (i, 0)),
                   pl.BlockSpec((H_B, tm), lambda i: (0, i))],
        out_shape=[jax.ShapeDtypeStruct((m, LANE), F32), jax.ShapeDtypeStruct((m, LANE), F32),
                   jax.ShapeDtypeStruct((H_B, m), F32)],
        scratch_shapes=[pltpu.VMEM((8, LANE), F32), pltpu.VMEM((H_B, LANE), F32)],
        compiler_params=_cparams(("arbitrary",)),
        name="forget_gates",
    )(x, wf_pad, wft, b_row, b_col)


def _online_update(s, v_bf, m_ref, l_ref, acc_ref, idx=None):
    sel = (lambda r: r) if idx is None else (lambda r: r.at[idx])
    m_prev = sel(m_ref)[...]
    m_new = jnp.maximum(m_prev, jnp.max(s, axis=-1, keepdims=True))
    alpha = jnp.exp(m_prev - m_new)
    p = jnp.exp(s - m_new)
    sel(l_ref)[...] = alpha * sel(l_ref)[...] + jnp.sum(p, axis=-1, keepdims=True)
    sel(acc_ref)[...] = alpha * sel(acc_ref)[...] + _dot(p.astype(BF16), v_bf)
    sel(m_ref)[...] = m_new


def _softmax_step(s, v_bf, m_prev, acc_prev, l_prev=None):
    m_new = jnp.maximum(m_prev, jnp.max(s, axis=-1, keepdims=True))
    alpha = jnp.exp(m_prev - m_new)
    p = jnp.exp(s - m_new)
    acc_new = alpha * acc_prev + _dot(p.astype(BF16), v_bf)
    if l_prev is None:
        return m_new, acc_new
    return m_new, acc_new, alpha * l_prev + jnp.sum(p, axis=-1, keepdims=True)


def _with_ones(v_bf):
    return jnp.concatenate([v_bf, jnp.ones_like(v_bf)], axis=1)


def _diff_lambda(lq1, lk1, lq2, lk2, lam_init):
    s1 = jnp.sum(lq1 * lk1, axis=-1, keepdims=True)
    s2 = jnp.sum(lq2 * lk2, axis=-1, keepdims=True)
    return jnp.exp(s1) - jnp.exp(s2) + lam_init


def _subln(o, g, lam_init):
    ms = jnp.mean(o * o, axis=-1, keepdims=True)
    return o * lax.rsqrt(ms + LN_EPS) * g * (1.0 - lam_init)


ROW_SPLIT = 2
FOX_SPLIT = 2
MLA_SPLIT = 2


def _causal_tile(t):
    return lax.broadcasted_iota(jnp.int32, (t, t), 0) >= lax.broadcasted_iota(jnp.int32, (t, t), 1)


def _pair_tables(n_tiles):
    qi = np.array([i for i in range(n_tiles) for _ in range(i + 1)], np.int32)
    kj = np.array([j for i in range(n_tiles) for j in range(i + 1)], np.int32)
    return jnp.asarray(qi), jnp.asarray(kj)


def _diff_prompt_body(qi_ref, kj_ref, sl_ref, q_ref, k_ref, v_ref, lq1, lk1, lq2, lk2, g_ref, o_ref,
                      m_ref, acc_ref, *, t, lam_init):
    h = pl.program_id(0)
    n = pl.program_id(1)
    i = qi_ref[n]
    j = kj_ref[n]

    @pl.when(j == 0)
    def _():
        m_ref[...] = jnp.full_like(m_ref, NEG_INF)
        acc_ref[...] = jnp.zeros_like(acc_ref)

    q = q_ref[...]
    lane = lax.broadcasted_iota(jnp.int32, q.shape, 1)
    kb = k_ref[...]
    v1 = _with_ones(v_ref[...])
    kcol = lax.broadcasted_iota(jnp.int32, (1, t), 1)
    bias = sl_ref[h] * ((j - i) * t + kcol).astype(F32)

    def update(masked):
        sub = t // ROW_SPLIT
        res = []
        for c in range(2):
            qc = jnp.where((lane >= c * DQK_A) & (lane < (c + 1) * DQK_A), q, jnp.zeros_like(q))
            for r in range(ROW_SPLIT):
                rows = slice(r * sub, (r + 1) * sub)
                sc = _dot_nt(qc[rows], kb) + bias
                if masked:
                    sc = jnp.where(_causal_tile(t)[rows], sc, NEG_INF)
                res.append((c, rows, _softmax_step(sc, v1, m_ref[c, rows], acc_ref[c, rows])))
        for c, rows, (m_new, acc_new) in res:
            m_ref[c, rows] = m_new
            acc_ref[c, rows] = acc_new

    @pl.when(j < i)
    def _():
        update(False)

    @pl.when(j == i)
    def _():
        update(True)
        lam = _diff_lambda(lq1[...], lk1[...], lq2[...], lk2[...], lam_init)
        a0, a1 = acc_ref[0], acc_ref[1]
        o = a0[:, :DV_A] / a0[:, DV_A:DV_A + 1] - lam * (a1[:, :DV_A] / a1[:, DV_A:DV_A + 1])
        o_ref[...] = _subln(o, g_ref[...], lam_init)


def diff_prompt(h, slopes, lq1, lk1, lq2, lk2, g_subln, lam_init, *, t=1024):
    s = h.shape[0]
    t = min(t, s)
    qi, kj = _pair_tables(s // t)
    cb = lambda off: off // LANE
    small = lambda hh, n, qi, kj, sl: (0, 0)
    return pl.pallas_call(
        functools.partial(_diff_prompt_body, t=t, lam_init=lam_init),
        grid_spec=pltpu.PrefetchScalarGridSpec(
            num_scalar_prefetch=3,
            grid=(H_A, qi.shape[0]),
            in_specs=[pl.BlockSpec((t, LANE), lambda hh, n, qi, kj, sl: (qi[n], cb(OFF_QD) + hh)),
                      pl.BlockSpec((t, LANE), lambda hh, n, qi, kj, sl: (kj[n], cb(OFF_KD) + hh // G_A)),
                      pl.BlockSpec((t, LANE), lambda hh, n, qi, kj, sl: (kj[n], cb(OFF_VD) + hh // G_A)),
                      pl.BlockSpec((1, DQK_A), small), pl.BlockSpec((1, DQK_A), small),
                      pl.BlockSpec((1, DQK_A), small), pl.BlockSpec((1, DQK_A), small),
                      pl.BlockSpec((1, DV_A), small)],
            out_specs=pl.BlockSpec((t, LANE), lambda hh, n, qi, kj, sl: (qi[n], hh)),
            scratch_shapes=[pltpu.VMEM((2, t, 1), F32), pltpu.VMEM((2, t, 2 * DV_A), F32)]),
        out_shape=jax.ShapeDtypeStruct((s, H_A * DV_A), F32),
        compiler_params=_cparams(("parallel", "arbitrary")),
        name="diff_prompt",
    )(qi, kj, slopes, h, h, h, lq1, lk1, lq2, lk2, g_subln)


def _fox_prompt_body(qi_ref, kj_ref, q_ref, k_ref, v_ref, cq_ref, ck_ref, o_ref, m_ref, acc_ref, *, t):
    h = pl.program_id(0)
    n = pl.program_id(1)
    i = qi_ref[n]
    j = kj_ref[n]

    @pl.when(j == 0)
    def _():
        m_ref[...] = jnp.full_like(m_ref, NEG_INF)
        acc_ref[...] = jnp.zeros_like(acc_ref)

    qb = q_ref[...]
    kb = k_ref[...]
    v1 = _with_ones(v_ref[...])
    lane = lax.broadcasted_iota(jnp.int32, (1, LANE), 1)
    base = jnp.sum(jnp.where(lane == h, cq_ref[0:1, :], 0.0), axis=-1, keepdims=True)
    bias = base - ck_ref[...]

    def update(masked):
        sub = t // FOX_SPLIT
        res = []
        for r in range(FOX_SPLIT):
            rows = slice(r * sub, (r + 1) * sub)
            sc = _dot_nt(qb[rows], kb) + bias
            if masked:
                sc = jnp.where(_causal_tile(t)[rows], sc, NEG_INF)
            res.append((rows, _softmax_step(sc, v1, m_ref[rows], acc_ref[rows])))
        for rows, (m_new, acc_new) in res:
            m_ref[rows] = m_new
            acc_ref[rows] = acc_new

    @pl.when(j < i)
    def _():
        update(False)

    @pl.when(j == i)
    def _():
        update(True)
        acc = acc_ref[...]
        o_ref[...] = acc[:, :D_B] / acc[:, D_B:D_B + 1]


def fox_prompt(h, cum, cumt3, *, t=1024):
    s = h.shape[0]
    t = min(t, s)
    qi, kj = _pair_tables(s // t)
    cb = lambda off: off // LANE
    return pl.pallas_call(
        functools.partial(_fox_prompt_body, t=t),
        grid_spec=pltpu.PrefetchScalarGridSpec(
            num_scalar_prefetch=2,
            grid=(H_B, qi.shape[0]),
            in_specs=[pl.BlockSpec((t, LANE), lambda hh, n, qi, kj: (qi[n], cb(OFF_QF) + hh)),
                      pl.BlockSpec((t, LANE), lambda hh, n, qi, kj: (kj[n], cb(OFF_KF) + hh // G_B)),
                      pl.BlockSpec((t, LANE), lambda hh, n, qi, kj: (kj[n], cb(OFF_VF) + hh // G_B)),
                      pl.BlockSpec((t, LANE), lambda hh, n, qi, kj: (qi[n], 0)),
                      pl.BlockSpec((None, 1, t), lambda hh, n, qi, kj: (hh, 0, kj[n]))],
            out_specs=pl.BlockSpec((t, LANE), lambda hh, n, qi, kj: (qi[n], hh)),
            scratch_shapes=[pltpu.VMEM((t, 1), F32), pltpu.VMEM((t, 2 * D_B), F32)]),
        out_shape=jax.ShapeDtypeStruct((s, H_B * D_B), F32),
        compiler_params=_cparams(("parallel", "arbitrary")),
        name="fox_prompt",
    )(qi, kj, h, h, h, cum, cumt3)


def _mla_prompt_body(qi_ref, kj_ref, q_ref, ckv_ref, kpe_ref, o_ref, m_ref, l_ref, acc_ref, *, t):
    n = pl.program_id(1)
    i = qi_ref[n]
    j = kj_ref[n]

    @pl.when(j == 0)
    def _():
        m_ref[...] = jnp.full_like(m_ref, NEG_INF)
        l_ref[...] = jnp.zeros_like(l_ref)
        acc_ref[...] = jnp.zeros_like(acc_ref)

    q = q_ref[...]
    ql = q[:, :KV_LORA]
    qp = q[:, KV_LORA:]
    ckv = ckv_ref[...]
    kpe = kpe_ref[...]

    def update(masked):
        sub = t // MLA_SPLIT
        res = []
        for r in range(MLA_SPLIT):
            rows = slice(r * sub, (r + 1) * sub)
            sc = _dot_nt(ql[rows], ckv) + _dot_nt(qp[rows], kpe)
            if masked:
                sc = jnp.where(_causal_tile(t)[rows], sc, NEG_INF)
            res.append((rows, _softmax_step(sc, ckv, m_ref[rows], acc_ref[rows], l_ref[rows])))
        for rows, (m_new, acc_new, l_new) in res:
            m_ref[rows] = m_new
            l_ref[rows] = l_new
            acc_ref[rows] = acc_new

    @pl.when(j < i)
    def _():
        update(False)

    @pl.when(j == i)
    def _():
        update(True)
        o_ref[...] = acc_ref[...] / l_ref[...]


def mla_prompt(q3, ckv, kpe, *, t=1024):
    s = ckv.shape[0]
    t = min(t, s)
    qi, kj = _pair_tables(s // t)
    return pl.pallas_call(
        functools.partial(_mla_prompt_body, t=t),
        grid_spec=pltpu.PrefetchScalarGridSpec(
            num_scalar_prefetch=2,
            grid=(H_C, qi.shape[0]),
            in_specs=[pl.BlockSpec((None, t, KV_LORA + ROPE_C), lambda hh, n, qi, kj: (hh, qi[n], 0)),
                      pl.BlockSpec((t, KV_LORA), lambda hh, n, qi, kj: (kj[n], 0)),
                      pl.BlockSpec((t, ROPE_C), lambda hh, n, qi, kj: (kj[n], 0))],
            out_specs=pl.BlockSpec((None, t, KV_LORA), lambda hh, n, qi, kj: (hh, qi[n], 0)),
            scratch_shapes=[pltpu.VMEM((t, 1), F32), pltpu.VMEM((t, 1), F32), pltpu.VMEM((t, KV_LORA), F32)]),
        out_shape=jax.ShapeDtypeStruct((H_C, s, KV_LORA), F32),
        compiler_params=_cparams(("parallel", "arbitrary")),
        name="mla_prompt",
    )(qi, kj, q3, ckv, kpe)


CHUNK_PAGES = 16
CHUNK = CHUNK_PAGES * PAGE_SIZE
T_S = 8


def _chunk_schedule(pt_ref, b, k, n_chunks, n_seq):
    c = n_chunks - 1 - k
    last = k == n_chunks - 1
    nb = jnp.where(last, b + 1, b)
    nc = jnp.where(last, n_chunks - 1, c - 1)
    return c, nb, nc, jnp.logical_or(jnp.logical_not(last), b + 1 < n_seq)


def _even_copies(pt_ref, caches, bufs, sem, b, c, slot):
    out = []
    for p in range(CHUNK_PAGES):
        page = pt_ref[b, c * CHUNK_PAGES + p]
        for a, (src, dst) in enumerate(zip(caches, bufs)):
            out.append(pltpu.make_async_copy(src.at[page], dst.at[slot, p], sem.at[slot, a]))
    return out


def _even_sample_body(pt_ref, hs_ref, cs_ref, cst_ref, lq1, lk1, lq2, lk2, g_ref,
                      dk_hbm, dv_hbm, fk_hbm, fv_hbm, lf_hbm, od_ref, of_ref,
                      dkbuf, dvbuf, fkbuf, fvbuf, lfbuf, sem, ktb, vdb, kfb, vfb,
                      md, ld, accd, mf, lff, accf, run_ref, *, n_chunks, n_seq, past_len, lam_init):
    b = pl.program_id(0)
    caches = (dk_hbm, dv_hbm, fk_hbm, fv_hbm, lf_hbm)
    bufs = (dkbuf, dvbuf, fkbuf, fvbuf, lfbuf)
    n_d, n_f = 2 * KV_A * G_A * T_S, KV_B * G_B * T_S

    @pl.when(b == 0)
    def _():
        for cp in _even_copies(pt_ref, caches, bufs, sem, 0, n_chunks - 1, 0):
            cp.start()

    hs = hs_ref[...]
    lane = lax.broadcasted_iota(jnp.int32, (T_S, LANE), 1)
    zero = jnp.zeros((T_S, LANE), F32)
    rows = []
    for kv in range(KV_A):
        for c in range(2):
            for g in range(G_A):
                qh = hs[:, OFF_QD + (kv * G_A + g) * LANE:OFF_QD + (kv * G_A + g + 1) * LANE] * DIFF_SCALE
                qh = jnp.where((lane >= c * DQK_A) & (lane < (c + 1) * DQK_A), qh, 0.0)
                rows.append(jnp.concatenate([qh, zero] if kv == 0 else [zero, qh], axis=1))
    qd = jnp.concatenate(rows, axis=0).astype(BF16)
    rows = []
    for hh in range(H_B):
        qh = hs[:, OFF_QF + hh * LANE:OFF_QF + (hh + 1) * LANE] * FOX_SCALE
        rows.append(jnp.concatenate([qh, zero] if hh // G_B == 0 else [zero, qh], axis=1))
    qf = jnp.concatenate(rows, axis=0).astype(BF16)

    rd = lax.broadcasted_iota(jnp.int32, (n_d, 1), 0)
    head_d = (rd // (2 * G_A * T_S)) * G_A + (rd // T_S) % G_A
    slope = jnp.exp2(-(head_d + 1).astype(F32))
    t_d = rd % T_S
    qpos_d = (past_len + t_d).astype(F32)
    rf = lax.broadcasted_iota(jnp.int32, (n_f, 1), 0)
    t_f = rf % T_S
    cs = cs_ref[...]
    cq = jnp.concatenate([cs[:, hh:hh + 1] for hh in range(H_B)], axis=0)

    md[...] = jnp.full_like(md, NEG_INF)
    ld[...] = jnp.zeros_like(ld)
    accd[...] = jnp.zeros_like(accd)
    mf[...] = jnp.full_like(mf, NEG_INF)
    lff[...] = jnp.zeros_like(lff)
    accf[...] = jnp.zeros_like(accf)
    run_ref[...] = jnp.zeros_like(run_ref)

    pad = jnp.zeros((LANE - T_S, 2 * LANE), F32)
    key = lax.broadcasted_iota(jnp.int32, (1, LANE), 1)
    kd_new = jnp.concatenate([hs[:, OFF_KD:OFF_KD + 2 * LANE], pad], axis=0).astype(BF16)
    vd_new = jnp.concatenate([hs[:, OFF_VD:OFF_VD + 2 * LANE], pad], axis=0).astype(BF16)
    kf_new = jnp.concatenate([hs[:, OFF_KF:OFF_KF + 2 * LANE], pad], axis=0).astype(BF16)
    vf_new = jnp.concatenate([hs[:, OFF_VF:OFF_VF + 2 * LANE], pad], axis=0).astype(BF16)
    s = _dot_nt(qd, kd_new) - slope * (t_d - key).astype(F32)
    _online_update(jnp.where(key <= t_d, s, NEG_INF), vd_new, md, ld, accd)
    cst = cst_ref[...]
    cst_rows = jnp.broadcast_to(cst[:, None, :], (H_B, T_S, LANE)).reshape(n_f, LANE)
    s = _dot_nt(qf, kf_new) + (cq - cst_rows)
    _online_update(jnp.where(key <= t_f, s, NEG_INF), vf_new, mf, lff, accf)

    jj = lax.broadcasted_iota(jnp.int32, (PAGE_SIZE, PAGE_SIZE), 0)
    ss = lax.broadcasted_iota(jnp.int32, (PAGE_SIZE, PAGE_SIZE), 1)
    later = jnp.where(jj > ss, 1.0, 0.0).astype(BF16)
    kidx = lax.broadcasted_iota(jnp.int32, (1, CHUNK), 1)

    def step(k, carry):
        c, nb, nc, has_next = _chunk_schedule(pt_ref, b, k, n_chunks, n_seq)
        slot = k % 2

        @pl.when(has_next)
        def _():
            for cp in _even_copies(pt_ref, caches, bufs, sem, nb, nc, 1 - slot):
                cp.start()

        for cp in _even_copies(pt_ref, caches, bufs, sem, b, c, slot):
            cp.wait()

        for p in range(CHUNK_PAGES):
            lo, hi = p * PAGE_SIZE, (p + 1) * PAGE_SIZE
            ktb[:, lo:hi] = dkbuf[slot, p].astype(BF16)
            for kv in range(2):
                sl = pl.ds(kv, PAGE_SIZE, stride=2)
                vdb[lo:hi, kv * LANE:(kv + 1) * LANE] = dvbuf[slot, p, sl, :].astype(BF16)
                kfb[lo:hi, kv * LANE:(kv + 1) * LANE] = fkbuf[slot, p, sl, :].astype(BF16)
                vfb[lo:hi, kv * LANE:(kv + 1) * LANE] = fvbuf[slot, p, sl, :].astype(BF16)

        kpos = (c * CHUNK + kidx).astype(F32)
        s = _dot(qd, ktb[...]) - slope * (qpos_d - kpos)
        _online_update(s, vdb[...], md, ld, accd)

        x = lfbuf[slot].reshape(CHUNK_PAGES * H_B, PAGE_SIZE)
        within = jnp.zeros_like(x)
        for part in _split3(x):
            within = within + _dot(part, later)
        tot = jnp.sum(x, axis=-1, keepdims=True)
        run = run_ref[:, 0:1]
        pieces = [None] * CHUNK_PAGES
        for p in reversed(range(CHUNK_PAGES)):
            after = within[p * H_B:(p + 1) * H_B] + run
            pieces[p] = jnp.broadcast_to(after[:, None, :], (H_B, T_S, PAGE_SIZE)).reshape(n_f, PAGE_SIZE)
            run = run + tot[p * H_B:(p + 1) * H_B]
        run_ref[...] = jnp.broadcast_to(run, run_ref.shape)
        s = _dot_nt(qf, kfb[...]) + (cq + jnp.concatenate(pieces, axis=1))
        _online_update(s, vfb[...], mf, lff, accf)
        return carry

    lax.fori_loop(0, n_chunks, step, 0)

    lam = _diff_lambda(lq1[...], lk1[...], lq2[...], lk2[...], lam_init)
    od = accd[...] / ld[...]
    g = g_ref[...]
    for kv in range(KV_A):
        for gg in range(G_A):
            r0 = ((kv * 2 + 0) * G_A + gg) * T_S
            r1 = ((kv * 2 + 1) * G_A + gg) * T_S
            o = od[r0:r0 + T_S, kv * LANE:(kv + 1) * LANE] - lam * od[r1:r1 + T_S, kv * LANE:(kv + 1) * LANE]
            od_ref[:, (kv * G_A + gg) * LANE:(kv * G_A + gg + 1) * LANE] = _subln(o, g, lam_init)
    of = accf[...] / lff[...]
    for hh in range(H_B):
        kv = hh // G_B
        of_ref[:, hh * LANE:(hh + 1) * LANE] = of[hh * T_S:(hh + 1) * T_S, kv * LANE:(kv + 1) * LANE]


def even_sample(page_table, hs, cs, cst, lq1, lk1, lq2, lk2, g_subln, dk_t, dv2, fk2, fv2, lf_t, lam_init):
    n_seq, n_pages = page_table.shape
    assert n_pages % (2 * CHUNK_PAGES) == 0 and hs.shape[0] == n_seq * T_S
    n_chunks = n_pages // CHUNK_PAGES
    width = hs.shape[1]
    small = lambda b, pt: (0, 0)
    anyspec = pl.BlockSpec(memory_space=pl.ANY)
    page_buf = lambda: pltpu.VMEM((2, CHUNK_PAGES, 2 * LANE, PAGE_SIZE), F32)
    n_d, n_f = 2 * KV_A * G_A * T_S, KV_B * G_B * T_S
    return pl.pallas_call(
        functools.partial(_even_sample_body, n_chunks=n_chunks, n_seq=n_seq, past_len=n_pages * PAGE_SIZE, lam_init=lam_init),
        grid_spec=pltpu.PrefetchScalarGridSpec(
            num_scalar_prefetch=1,
            grid=(n_seq,),
            in_specs=[pl.BlockSpec((T_S, width), lambda b, pt: (b, 0)),
                      pl.BlockSpec((T_S, LANE), lambda b, pt: (b, 0)),
                      pl.BlockSpec((None, H_B, LANE), lambda b, pt: (b, 0, 0)),
                      pl.BlockSpec((1, DQK_A), small), pl.BlockSpec((1, DQK_A), small),
                      pl.BlockSpec((1, DQK_A), small), pl.BlockSpec((1, DQK_A), small),
                      pl.BlockSpec((1, DV_A), small),
                      anyspec, anyspec, anyspec, anyspec, anyspec],
            out_specs=[pl.BlockSpec((T_S, H_A * DV_A), lambda b, pt: (b, 0)),
                       pl.BlockSpec((T_S, H_B * D_B), lambda b, pt: (b, 0))],
            scratch_shapes=[page_buf(), page_buf(), page_buf(), page_buf(),
                            pltpu.VMEM((2, CHUNK_PAGES, H_B, PAGE_SIZE), F32),
                            pltpu.SemaphoreType.DMA((2, 5)),
                            pltpu.VMEM((2 * LANE, CHUNK), BF16), pltpu.VMEM((CHUNK, 2 * LANE), BF16),
                            pltpu.VMEM((CHUNK, 2 * LANE), BF16), pltpu.VMEM((CHUNK, 2 * LANE), BF16),
                            pltpu.VMEM((n_d, 1), F32), pltpu.VMEM((n_d, 1), F32), pltpu.VMEM((n_d, 2 * LANE), F32),
                            pltpu.VMEM((n_f, 1), F32), pltpu.VMEM((n_f, 1), F32), pltpu.VMEM((n_f, 2 * LANE), F32),
                            pltpu.VMEM((H_B, LANE), F32)]),
        out_shape=[jax.ShapeDtypeStruct((n_seq * T_S, H_A * DV_A), F32), jax.ShapeDtypeStruct((n_seq * T_S, H_B * D_B), F32)],
        compiler_params=_cparams(("arbitrary",)),
        name="even_sample",
    )(page_table, hs, cs, cst, lq1, lk1, lq2, lk2, g_subln, dk_t, dv2, fk2, fv2, lf_t)


MLA_CHUNK_PAGES = 32
MLA_CHUNK = MLA_CHUNK_PAGES * PAGE_SIZE


def _mla_copies(pt_ref, caches, bufs, sem, b, c, slot):
    out = []
    for p in range(MLA_CHUNK_PAGES):
        page = pt_ref[b, c * MLA_CHUNK_PAGES + p]
        for a, (src, dst) in enumerate(zip(caches, bufs)):
            out.append(pltpu.make_async_copy(src.at[page], dst.at[slot, p], sem.at[slot, a]))
    return out


def _mla_sample_body(pt_ref, q_ref, ckvn_ref, kpen_ref, ckv_hbm, kpe_hbm, o_ref,
                     ckvbuf, kpebuf, sem, ckvb, kpeb, m_ref, l_ref, acc_ref, *, n_chunks, n_seq):
    b = pl.program_id(0)
    caches = (ckv_hbm, kpe_hbm)
    bufs = (ckvbuf, kpebuf)
    n_r = H_C * T_S

    @pl.when(b == 0)
    def _():
        for cp in _mla_copies(pt_ref, caches, bufs, sem, 0, n_chunks - 1, 0):
            cp.start()

    q = q_ref[...].reshape(n_r, KV_LORA + ROPE_C) * MLA_SCALE
    ql = q[:, :KV_LORA].astype(BF16)
    qp = q[:, KV_LORA:].astype(BF16)
    t_r = lax.broadcasted_iota(jnp.int32, (n_r, 1), 0) % T_S
    key = lax.broadcasted_iota(jnp.int32, (1, LANE), 1)

    m_ref[...] = jnp.full_like(m_ref, NEG_INF)
    l_ref[...] = jnp.zeros_like(l_ref)
    acc_ref[...] = jnp.zeros_like(acc_ref)

    ckv_new = jnp.concatenate([ckvn_ref[...], jnp.zeros((LANE - T_S, KV_LORA), F32)], axis=0).astype(BF16)
    kpe_new = jnp.concatenate([kpen_ref[...], jnp.zeros((LANE - T_S, ROPE_C), F32)], axis=0).astype(BF16)
    s = _dot_nt(ql, ckv_new) + _dot_nt(qp, kpe_new)
    _online_update(jnp.where(key <= t_r, s, NEG_INF), ckv_new, m_ref, l_ref, acc_ref)

    def step(k, carry):
        c, nb, nc, has_next = _chunk_schedule(pt_ref, b, k, n_chunks, n_seq)
        slot = k % 2

        @pl.when(has_next)
        def _():
            for cp in _mla_copies(pt_ref, caches, bufs, sem, nb, nc, 1 - slot):
                cp.start()

        for cp in _mla_copies(pt_ref, caches, bufs, sem, b, c, slot):
            cp.wait()

        for p in range(MLA_CHUNK_PAGES):
            lo, hi = p * PAGE_SIZE, (p + 1) * PAGE_SIZE
            ckvb[lo:hi, :] = ckvbuf[slot, p].astype(BF16)
            kpeb[:, lo:hi] = kpebuf[slot, p].astype(BF16)

        kv = ckvb[...]
        s = _dot_nt(ql, kv) + _dot(qp, kpeb[...])
        _online_update(s, kv, m_ref, l_ref, acc_ref)
        return carry

    lax.fori_loop(0, n_chunks, step, 0)
    o_ref[...] = (acc_ref[...] / l_ref[...]).reshape(H_C, T_S, KV_LORA)


def mla_sample(page_table, q3, ckv_new, kpe_new, ckv_pages, kpe_t):
    n_seq, n_pages = page_table.shape
    assert n_pages % (2 * MLA_CHUNK_PAGES) == 0
    n_chunks = n_pages // MLA_CHUNK_PAGES
    anyspec = pl.BlockSpec(memory_space=pl.ANY)
    n_r = H_C * T_S
    return pl.pallas_call(
        functools.partial(_mla_sample_body, n_chunks=n_chunks, n_seq=n_seq),
        grid_spec=pltpu.PrefetchScalarGridSpec(
            num_scalar_prefetch=1,
            grid=(n_seq,),
            in_specs=[pl.BlockSpec((H_C, T_S, KV_LORA + ROPE_C), lambda b, pt: (0, b, 0)),
                      pl.BlockSpec((T_S, KV_LORA), lambda b, pt: (b, 0)),
                      pl.BlockSpec((T_S, ROPE_C), lambda b, pt: (b, 0)),
                      anyspec, anyspec],
            out_specs=pl.BlockSpec((H_C, T_S, KV_LORA), lambda b, pt: (0, b, 0)),
            scratch_shapes=[pltpu.VMEM((2, MLA_CHUNK_PAGES, PAGE_SIZE, KV_LORA), F32),
                            pltpu.VMEM((2, MLA_CHUNK_PAGES, ROPE_C, PAGE_SIZE), F32),
                            pltpu.SemaphoreType.DMA((2, 2)),
                            pltpu.VMEM((MLA_CHUNK, KV_LORA), BF16), pltpu.VMEM((ROPE_C, MLA_CHUNK), BF16),
                            pltpu.VMEM((n_r, 1), F32), pltpu.VMEM((n_r, 1), F32), pltpu.VMEM((n_r, KV_LORA), F32)]),
        out_shape=jax.ShapeDtypeStruct((H_C, n_seq * T_S, KV_LORA), F32),
        compiler_params=_cparams(("arbitrary",)),
        name="mla_sample",
    )(page_table, q3, ckv_new, kpe_new, ckv_pages, kpe_t)


def _rope(x, cos2, sin2):
    half = x.shape[-1] // 2
    swapped = jnp.concatenate([x[:, half:], x[:, :half]], axis=1)
    return x * cos2 + swapped * sin2


def _rms(v, g, eps):
    return v * lax.rsqrt(jnp.mean(v * v, axis=-1, keepdims=True) + eps) * g


def _mla_cq_body(x_ref, w_ref, g_ref, o_ref):
    cq = _dot(x_ref[...].astype(BF16), w_ref[...].astype(BF16))
    o_ref[...] = _rms(cq, g_ref[...], RMS_EPS).astype(o_ref.dtype)


def mla_cq(x, w_dq, g_qn, *, tm=512):
    m, k = x.shape
    tm = min(tm, m)
    return pl.pallas_call(
        _mla_cq_body,
        grid=(m // tm,),
        in_specs=[pl.BlockSpec((tm, k), lambda i: (i, 0)), pl.BlockSpec((k, Q_LORA), lambda i: (0, 0)),
                  pl.BlockSpec((1, Q_LORA), lambda i: (0, 0))],
        out_specs=pl.BlockSpec((tm, Q_LORA), lambda i: (i, 0)),
        out_shape=jax.ShapeDtypeStruct((m, Q_LORA), BF16),
        compiler_params=_cparams(("parallel",)),
        name="mla_cq",
    )(x, w_dq, g_qn)


def _mla_q_body(cq_ref, wuq_ref, wuk_ref, cos_ref, sin_ref, o_ref, *, scale):
    q = _dot_nt(cq_ref[...].astype(BF16), wuq_ref[...].astype(BF16))
    q_lat = _dot_nt(q[:, :NOPE_C].astype(BF16), wuk_ref[...].astype(BF16))
    o_ref[:, :KV_LORA] = (q_lat * scale).astype(o_ref.dtype)
    o_ref[:, KV_LORA:] = (_rope(q[:, NOPE_C:], cos_ref[...], sin_ref[...]) * scale).astype(o_ref.dtype)


def mla_q(cq, wuq_t, w_ukv2, cos2, sin2, *, scale=1.0, out_dtype=F32, tm=512):
    m = cq.shape[0]
    tm = min(tm, m)
    return pl.pallas_call(
        functools.partial(_mla_q_body, scale=scale),
        grid=(H_C, m // tm),
        in_specs=[pl.BlockSpec((tm, Q_LORA), lambda h, i: (i, 0)),
                  pl.BlockSpec((None, NOPE_C + ROPE_C, Q_LORA), lambda h, i: (h, 0, 0)),
                  pl.BlockSpec((KV_LORA, NOPE_C), lambda h, i: (0, 2 * h)),
                  pl.BlockSpec((tm, ROPE_C), lambda h, i: (i, 0)), pl.BlockSpec((tm, ROPE_C), lambda h, i: (i, 0))],
        out_specs=pl.BlockSpec((None, tm, KV_LORA + ROPE_C), lambda h, i: (h, i, 0)),
        out_shape=jax.ShapeDtypeStruct((H_C, m, KV_LORA + ROPE_C), out_dtype),
        compiler_params=_cparams(("parallel", "parallel")),
        name="mla_q",
    )(cq, wuq_t, w_ukv2, cos2, sin2)


def _mla_kv_body(x_ref, w_ref, g_ref, cos_ref, sin_ref, ckv_ref, kpe_ref, ckvb_ref, kpeb_ref):
    kv = _dot_nt(x_ref[...].astype(BF16), w_ref[...].astype(BF16))
    ckv = _rms(kv[:, :KV_LORA], g_ref[...], RMS_EPS)
    kpe = _rope(kv[:, KV_LORA:], cos_ref[...], sin_ref[...])
    ckv_ref[...] = ckv
    kpe_ref[...] = kpe
    ckvb_ref[...] = ckv.astype(BF16)
    kpeb_ref[...] = kpe.astype(BF16)


def mla_kv(x, wdkv_t, g_kvn, cos2, sin2, *, tm=512):
    m, k = x.shape
    tm = min(tm, m)
    return pl.pallas_call(
        _mla_kv_body,
        grid=(m // tm,),
        in_specs=[pl.BlockSpec((tm, k), lambda i: (i, 0)), pl.BlockSpec((KV_LORA + ROPE_C, k), lambda i: (0, 0)),
                  pl.BlockSpec((1, KV_LORA), lambda i: (0, 0)),
                  pl.BlockSpec((tm, ROPE_C), lambda i: (i, 0)), pl.BlockSpec((tm, ROPE_C), lambda i: (i, 0))],
        out_specs=[pl.BlockSpec((tm, KV_LORA), lambda i: (i, 0)), pl.BlockSpec((tm, ROPE_C), lambda i: (i, 0)),
                   pl.BlockSpec((tm, KV_LORA), lambda i: (i, 0)), pl.BlockSpec((tm, ROPE_C), lambda i: (i, 0))],
        out_shape=[jax.ShapeDtypeStruct((m, KV_LORA), F32), jax.ShapeDtypeStruct((m, ROPE_C), F32),
                   jax.ShapeDtypeStruct((m, KV_LORA), BF16), jax.ShapeDtypeStruct((m, ROPE_C), BF16)],
        compiler_params=_cparams(("parallel",)),
        name="mla_kv",
    )(x, wdkv_t, g_kvn, cos2, sin2)


def _mla_ov_body(o_ref, wuv_ref, out_ref):
    out_ref[...] = _dot(o_ref[...].astype(BF16), wuv_ref[...].astype(BF16))


def mla_ov(o3, w_ukv2, *, tm=512):
    m = o3.shape[1]
    tm = min(tm, m)
    return pl.pallas_call(
        _mla_ov_body,
        grid=(H_C, m // tm),
        in_specs=[pl.BlockSpec((None, tm, KV_LORA), lambda h, i: (h, i, 0)),
                  pl.BlockSpec((KV_LORA, V_C), lambda h, i: (0, 2 * h + 1))],
        out_specs=pl.BlockSpec((tm, V_C), lambda h, i: (i, h)),
        out_shape=jax.ShapeDtypeStruct((m, H_C * V_C), F32),
        compiler_params=_cparams(("parallel", "parallel")),
        name="mla_ov",
    )(o3, w_ukv2)


def _router_body(x_ref, w_ref, b_ref, gates_ref):
    x = x_ref[...]
    w = w_ref[...]
    xh = x.astype(BF16)
    xl = (x - xh.astype(F32)).astype(BF16)
    wh = w.astype(BF16)
    wl = (w - wh.astype(F32)).astype(BF16)
    logit = _dot(xh, wh) + _dot(xh, wl) + _dot(xl, wh) + b_ref[...]
    lane = lax.broadcasted_iota(jnp.int32, logit.shape, 1)
    big = jnp.int32(2 ** 30)

    def first_max(mask):
        v = jnp.max(jnp.where(mask, logit, NEG_INF), axis=-1, keepdims=True)
        idx = jnp.min(jnp.where(mask & (logit == v), lane, big), axis=-1, keepdims=True)
        return v, idx

    is_group = (lane >= N_EXPERTS) & (lane < N_EXPERTS + N_GROUPS)
    gmax, gidx = first_max(is_group)
    w_grp = 1.0 / jnp.sum(jnp.where(is_group, jnp.exp(logit - gmax), 0.0), axis=-1, keepdims=True)
    g_sel = gidx - N_EXPERTS
    in_grp = (lane >= g_sel * EXP_PER_GROUP) & (lane < (g_sel + 1) * EXP_PER_GROUP)
    v1, i1 = first_max(in_grp)
    v2, i2 = first_max(in_grp & (lane != i1))
    e2 = jnp.exp(v2 - v1)
    w1 = w_grp / (1.0 + e2)
    w2 = w_grp * e2 / (1.0 + e2)
    route = jnp.where(lane == 0, i1.astype(F32), 0.0) + jnp.where(lane == 1, i2.astype(F32), 0.0)
    gates_ref[...] = route + jnp.where(lane == 2, w1, 0.0) + jnp.where(lane == 3, w2, 0.0)


def router(x, w_pad, b_pad, *, tm=512):
    m, k = x.shape
    tm = min(tm, m)
    return pl.pallas_call(
        _router_body,
        grid=(m // tm,),
        in_specs=[pl.BlockSpec((tm, k), lambda i: (i, 0)), pl.BlockSpec((k, LANE), lambda i: (0, 0)),
                  pl.BlockSpec((1, LANE), lambda i: (0, 0))],
        out_specs=pl.BlockSpec((tm, LANE), lambda i: (i, 0)),
        out_shape=jax.ShapeDtypeStruct((m, LANE), F32),
        compiler_params=_cparams(("parallel",)),
        name="router",
    )(x, w_pad, b_pad)


MOE_TILE = 256
ROW_UNROLL = 8


def _row_gather(idx_smem, islot, src_hbm, dst, dslot, sem, n_rows, wait):
    def body(blk, carry):
        for k in range(ROW_UNROLL):
            r = blk * ROW_UNROLL + k
            row = idx_smem[islot, r]
            cp = pltpu.make_async_copy(src_hbm.at[pl.ds(row, 1), :], dst.at[dslot, pl.ds(r, 1), :], sem.at[dslot])
            cp.wait() if wait else cp.start(priority=k % 2)
        return carry
    lax.fori_loop(0, n_rows // ROW_UNROLL, body, 0)


def _gather_pipeline(t, n_act, idx_hbm, idx_smem, isem, src_hbm, dst, gsem, n_rows):
    idx_copy = lambda tile: pltpu.make_async_copy(idx_hbm.at[tile], idx_smem.at[tile % 3], isem.at[tile % 3])

    @pl.when(t == 0)
    def _():
        idx_copy(0).start()
        idx_copy(0).wait()
        _row_gather(idx_smem, 0, src_hbm, dst, 0, gsem, n_rows, wait=False)

        @pl.when(n_act > 1)
        def _():
            idx_copy(1).start()

    @pl.when(t + 1 < n_act)
    def _():
        idx_copy(t + 1).wait()
        _row_gather(idx_smem, (t + 1) % 3, src_hbm, dst, (t + 1) % 2, gsem, n_rows, wait=False)

        @pl.when(t + 2 < n_act)
        def _():
            idx_copy(t + 2).start()

    @pl.when(t < n_act)
    def _():
        _row_gather(idx_smem, t % 3, src_hbm, dst, t % 2, gsem, n_rows, wait=True)


def _moe_ffn_body(te_ref, na_ref, tok_hbm, x_hbm, gate_ref, wg_ref, wu_ref, wd_ref, o_ref,
                  xbuf, tok_smem, gsem, isem, wgb, wub, wdb):
    t = pl.program_id(0)
    n_act = na_ref[0]
    _gather_pipeline(t, n_act, tok_hbm, tok_smem, isem, x_hbm, xbuf, gsem, MOE_TILE)

    @pl.when(t < n_act)
    def _():
        @pl.when((t == 0) | (te_ref[t] != te_ref[jnp.maximum(t - 1, 0)]))
        def _():
            wgb[...] = wg_ref[...].astype(BF16)
            wub[...] = wu_ref[...].astype(BF16)
            wdb[...] = wd_ref[...].astype(BF16)

        xb = xbuf[t % 2].astype(BF16)
        a = _dot(xb, wgb[...])
        u = _dot(xb, wub[...])
        hmid = a * (1.0 / (1.0 + jnp.exp(-a))) * u * gate_ref[...]
        o_ref[...] = _dot(hmid.astype(BF16), wdb[...])

    @pl.when(t >= n_act)
    def _():
        o_ref[...] = jnp.zeros_like(o_ref)


def moe_ffn(tile_expert, n_active, tok_tiles, x, gate_col, w_gate, w_up, w_down, layer):
    n_tiles = tok_tiles.shape[0]
    d = x.shape[1]
    wspec = lambda shape: pl.BlockSpec((None, None) + shape, lambda t, te, na: (layer, te[t], 0, 0))
    return pl.pallas_call(
        _moe_ffn_body,
        grid_spec=pltpu.PrefetchScalarGridSpec(
            num_scalar_prefetch=2,
            grid=(n_tiles,),
            in_specs=[pl.BlockSpec(memory_space=pl.ANY), pl.BlockSpec(memory_space=pl.ANY),
                      pl.BlockSpec((MOE_TILE, 1), lambda t, te, na: (t, 0)),
                      wspec((d, D_EXPERT)), wspec((d, D_EXPERT)), wspec((D_EXPERT, d))],
            out_specs=pl.BlockSpec((MOE_TILE, d), lambda t, te, na: (t, 0)),
            scratch_shapes=[pltpu.VMEM((2, MOE_TILE, d), F32), pltpu.SMEM((3, MOE_TILE), jnp.int32),
                            pltpu.SemaphoreType.DMA((2,)), pltpu.SemaphoreType.DMA((3,)),
                            pltpu.VMEM((d, D_EXPERT), BF16), pltpu.VMEM((d, D_EXPERT), BF16),
                            pltpu.VMEM((D_EXPERT, d), BF16)]),
        out_shape=jax.ShapeDtypeStruct((n_tiles * MOE_TILE, d), F32),
        compiler_params=_cparams(("arbitrary",)),
        name="moe_ffn",
    )(tile_expert, n_active, tok_tiles, x, gate_col, w_gate, w_up, w_down)


def _moe_combine_body(pos_hbm, y_hbm, x_ref, g_ref, b_ref, o_ref, ybuf, pos_smem, gsem, isem, *, tm):
    t = pl.program_id(0)
    _gather_pipeline(t, pl.num_programs(0), pos_hbm, pos_smem, isem, y_hbm, ybuf, gsem, 2 * tm)
    y = ybuf[t % 2, :tm] + ybuf[t % 2, tm:]
    o_ref[...] = _layer_norm(ALPHA * x_ref[...] + y, g_ref[...], b_ref[...])


def moe_combine_ln(pos_tiles, y_sorted, x, g, b, *, tm):
    m, d = x.shape
    row = lambda i: (i, 0)
    fixed = lambda i: (0, 0)
    return pl.pallas_call(
        functools.partial(_moe_combine_body, tm=tm),
        grid=(m // tm,),
        in_specs=[pl.BlockSpec(memory_space=pl.ANY), pl.BlockSpec(memory_space=pl.ANY),
                  pl.BlockSpec((tm, d), row), pl.BlockSpec((1, d), fixed), pl.BlockSpec((1, d), fixed)],
        out_specs=pl.BlockSpec((tm, d), row),
        out_shape=jax.ShapeDtypeStruct((m, d), F32),
        scratch_shapes=[pltpu.VMEM((2, 2 * tm, d), F32), pltpu.SMEM((3, 2 * tm), jnp.int32),
                        pltpu.SemaphoreType.DMA((2,)), pltpu.SemaphoreType.DMA((3,))],
        compiler_params=_cparams(("arbitrary",)),
        name="moe_combine_ln",
    )(pos_tiles, y_sorted, x, g, b)


def _routing_tables(route, tm_tok):
    n = route.shape[0]
    n_asg = 2 * n
    n_tiles = (n_asg + N_EXPERTS * MOE_TILE) // MOE_TILE
    ef = route[:, 0:2].astype(jnp.int32).reshape(n_asg)
    wf = route[:, 2:4].reshape(n_asg)
    order = jnp.argsort(ef, stable=True).astype(jnp.int32)
    e_sorted = ef[order]
    bounds = jnp.searchsorted(e_sorted, jnp.arange(N_EXPERTS + 1, dtype=jnp.int32), side="left").astype(jnp.int32)
    counts = bounds[1:] - bounds[:-1]
    padded = ((counts + MOE_TILE - 1) // MOE_TILE) * MOE_TILE
    seg_end_pad = jnp.cumsum(padded)
    seg_start_pad = seg_end_pad - padded
    seg_start = jnp.cumsum(counts) - counts
    n_active = (seg_end_pad[-1] // MOE_TILE).astype(jnp.int32)
    tile_start = jnp.arange(n_tiles, dtype=jnp.int32) * MOE_TILE
    tile_expert = jnp.minimum(jnp.searchsorted(seg_end_pad, tile_start, side="right"), N_EXPERTS - 1).astype(jnp.int32)
    tile_expert = jnp.where(jnp.arange(n_tiles) < n_active, tile_expert, tile_expert[jnp.maximum(n_active - 1, 0)])
    p = jnp.arange(n_tiles * MOE_TILE, dtype=jnp.int32)
    e_p = jnp.repeat(tile_expert, MOE_TILE)
    local = p - seg_start_pad[e_p]
    valid = (local < counts[e_p]) & (p < seg_end_pad[-1])
    rank = jnp.clip(seg_start[e_p] + local, 0, n_asg - 1)
    asg = order[rank]
    row_token = jnp.where(valid, asg // 2, 0).astype(jnp.int32)
    row_gate = jnp.where(valid, wf[asg], 0.0)
    ppos = seg_start_pad[e_sorted] + (jnp.arange(n_asg, dtype=jnp.int32) - seg_start[e_sorted])
    pos_a = jnp.zeros((n_asg,), jnp.int32).at[order].set(ppos.astype(jnp.int32), unique_indices=True)
    pos_tiles = jnp.transpose(pos_a.reshape(n // tm_tok, tm_tok, 2), (0, 2, 1)).reshape(n // tm_tok, 2 * tm_tok)
    return (tile_expert, n_active.reshape(1), row_token.reshape(n_tiles, MOE_TILE),
            row_gate.reshape(n_tiles * MOE_TILE, 1), pos_tiles)


def _rope_tables(pos):
    half = ROPE_C // 2
    freqs = ROPE_THETA ** (-jnp.arange(half, dtype=F32) / half)
    ang = pos.astype(F32)[:, None] * freqs
    cos, sin = jnp.cos(ang), jnp.sin(ang)
    return jnp.concatenate([cos, cos], axis=1), jnp.concatenate([-sin, sin], axis=1)


def _lambda_init(layer_idx):
    return 0.8 - 0.6 * math.exp(-0.3 * layer_idx)


def _row(v):
    return v.reshape(1, -1)


def _moe_ln_block(x, l, g, b, w_router_group, b_router_group, w_router_expert, b_router_expert, w_gate, w_up, w_down):
    pad = jnp.zeros((D_MODEL, LANE - N_EXPERTS - N_GROUPS), F32)
    w_pad = jnp.concatenate([w_router_expert[l], w_router_group[l], pad], axis=1)
    b_pad = jnp.concatenate([b_router_expert[l], b_router_group[l], jnp.zeros((LANE - N_EXPERTS - N_GROUPS,), F32)])[None]
    tm_tok = math.gcd(x.shape[0], MOE_TILE)
    route = router(x, w_pad, b_pad, tm=tm_tok)
    tile_expert, n_active, tok_tiles, gate_col, pos_tiles = _routing_tables(route, tm_tok)
    y_sorted = moe_ffn(tile_expert, n_active, tok_tiles, x, gate_col, w_gate, w_up, w_down, l)
    return moe_combine_ln(pos_tiles, y_sorted, x, g, b, tm=tm_tok)


def kernel(x_prompt, x_sample, cache_diff_k, cache_diff_v, cache_fox_k, cache_fox_v, cache_fox_logf, cache_mla_ckv, cache_mla_kpe, page_table, w_in_even, b_forget, lambda_q1, lambda_k1, lambda_q2, lambda_k2, g_subln, w_out_even, w_dq, g_q_norm, w_uq, w_dkv, g_kv_norm, w_ukv, w_o_mla, ln_attn_g, ln_attn_b, ln_ffn_g, ln_ffn_b, w_router_group, b_router_group, w_router_expert, b_router_expert, w_gate, w_up, w_down):
    n_b, s_len, d = x_prompt.shape
    n_seq, t_new, _ = x_sample.shape
    assert n_b == 1 and t_new == T_S
    n_pool = cache_diff_k.shape[1]
    past = page_table.shape[1] * PAGE_SIZE
    xp = x_prompt.reshape(s_len, d)
    xs = x_sample.reshape(n_seq * T_S, d)
    slopes = 2.0 ** (-8.0 * jnp.arange(1, H_A + 1, dtype=F32) / H_A)
    col = np.arange(OFF_FL)
    col_scale = jnp.asarray(np.where(col < OFF_KD, DIFF_SCALE, np.where((col >= OFF_QF) & (col < OFF_KF), FOX_SCALE, 1.0)),
                            F32).reshape(1, OFF_FL)
    moe_w = (w_router_group, b_router_group, w_router_expert, b_router_expert, w_gate, w_up, w_down)
    outs_p, outs_s = {}, {}

    for l in range(DEPTH):
        i = l // 2
        if l % 2 == 0:
            lam_init = _lambda_init(l)
            w_t = jnp.transpose(w_in_even[i])
            wft = w_t[OFF_FL:OFF_FL + H_B]
            wf_pad = jnp.concatenate([jnp.transpose(wft), jnp.zeros((d, LANE - H_B), F32)], axis=1)
            b_row = jnp.concatenate([b_forget[i], jnp.zeros((LANE - H_B,), F32)])[None]
            b_col = b_forget[i][:, None]
            lam_args = (_row(lambda_q1[i]), _row(lambda_k1[i]), _row(lambda_q2[i]), _row(lambda_k2[i]), _row(g_subln[i]))
            hp, hp_b = matmul_scaled(xp, w_t, col_scale, nt=True, n_out=OFF_FL, name="even_proj_p")
            lf_p, cum_p, cumt_p = forget_gates(xp, wf_pad, wft, b_row, b_col, seg=s_len)
            od = diff_prompt(hp_b, slopes, *lam_args, lam_init)
            of = fox_prompt(hp_b, cum_p, cumt_p.reshape(H_B, 1, s_len))
            xp = matmul_ln(jnp.concatenate([od, of], axis=1), w_out_even[i], xp,
                           _row(ln_attn_g[l]), _row(ln_attn_b[l]), name="even_out_p")
            outs_p[l] = (hp[:, OFF_KD:OFF_VD].reshape(1, s_len, KV_A, 2, DQK_A), hp[:, OFF_VD:OFF_QF].reshape(1, s_len, KV_A, DV_A),
                         hp[:, OFF_KF:OFF_VF].reshape(1, s_len, KV_B, D_B), hp[:, OFF_VF:OFF_FL].reshape(1, s_len, KV_B, D_B),
                         lf_p[:, :H_B].reshape(1, s_len, H_B))
            hs, _ = matmul_scaled(xs, w_t, col_scale, nt=True, n_out=OFF_FL, name="even_proj_s")
            lf_s, cum_s, cumt_s = forget_gates(xs, wf_pad, wft, b_row, b_col, seg=T_S)
            cst = jnp.transpose(cumt_s.reshape(H_B, n_seq, T_S), (1, 0, 2))
            cst = jnp.concatenate([cst, jnp.zeros((n_seq, H_B, LANE - T_S), F32)], axis=2)
            dk_t = jnp.transpose(cache_diff_k[i], (0, 2, 3, 4, 1)).reshape(n_pool, KV_A * 2 * DQK_A, PAGE_SIZE)
            dv2 = cache_diff_v[i].reshape(n_pool, PAGE_SIZE * KV_A, DV_A)
            fk2 = cache_fox_k[i].reshape(n_pool, PAGE_SIZE * KV_B, D_B)
            fv2 = cache_fox_v[i].reshape(n_pool, PAGE_SIZE * KV_B, D_B)
            lf_t = jnp.transpose(cache_fox_logf[i], (0, 2, 1))
            od, of = even_sample(page_table, hs, cum_s, cst, *lam_args, dk_t, dv2, fk2, fv2, lf_t, lam_init)
            xs = matmul_ln(jnp.concatenate([od, of], axis=1), w_out_even[i], xs,
                           _row(ln_attn_g[l]), _row(ln_attn_b[l]), name="even_out_s")
            outs_s[l] = (hs[:, OFF_KD:OFF_VD].reshape(n_seq, T_S, KV_A, 2, DQK_A), hs[:, OFF_VD:OFF_QF].reshape(n_seq, T_S, KV_A, DV_A),
                         hs[:, OFF_KF:OFF_VF].reshape(n_seq, T_S, KV_B, D_B), hs[:, OFF_VF:OFF_FL].reshape(n_seq, T_S, KV_B, D_B),
                         lf_s[:, :H_B].reshape(n_seq, T_S, H_B))
        else:
            wuq_t = jnp.transpose(w_uq[i], (1, 2, 0))
            wdkv_t = jnp.transpose(w_dkv[i])
            w_ukv2 = w_ukv[i].reshape(KV_LORA, H_C * (NOPE_C + V_C))
            kpe_t = jnp.transpose(cache_mla_kpe[i], (0, 2, 1))
            for grp in ("p", "s"):
                x = xp if grp == "p" else xs
                pos = jnp.arange(s_len, dtype=jnp.int32) if grp == "p" else past + jnp.tile(jnp.arange(T_S, dtype=jnp.int32), n_seq)
                cos2, sin2 = _rope_tables(pos)
                cq = mla_cq(x, w_dq[i], _row(g_q_norm[i]))
                ckv, kpe, ckv_b, kpe_b = mla_kv(x, wdkv_t, _row(g_kv_norm[i]), cos2, sin2)
                if grp == "p":
                    q3 = mla_q(cq, wuq_t, w_ukv2, cos2, sin2, scale=MLA_SCALE, out_dtype=BF16)
                    o3 = mla_prompt(q3, ckv_b, kpe_b)
                else:
                    q3 = mla_q(cq, wuq_t, w_ukv2, cos2, sin2)
                    o3 = mla_sample(page_table, q3, ckv, kpe, cache_mla_ckv[i], kpe_t)
                ov = mla_ov(o3, w_ukv2)
                x = matmul_ln(ov, w_o_mla[i], x, _row(ln_attn_g[l]), _row(ln_attn_b[l]), name="mla_out_" + grp)
                if grp == "p":
                    xp = x
                    outs_p[l] = (ckv.reshape(1, s_len, KV_LORA), kpe.reshape(1, s_len, ROPE_C))
                else:
                    xs = x
                    outs_s[l] = (ckv.reshape(n_seq, T_S, KV_LORA), kpe.reshape(n_seq, T_S, ROPE_C))
        x_all = _moe_ln_block(jnp.concatenate([xp, xs], axis=0), l, _row(ln_ffn_g[l]), _row(ln_ffn_b[l]), *moe_w)
        xp, xs = x_all[:s_len], x_all[s_len:]

    stack = lambda outs, k, ls: jnp.stack([outs[l][k] for l in ls])
    even, odd = range(0, DEPTH, 2), range(1, DEPTH, 2)
    return (xp.reshape(1, s_len, d), xs.reshape(n_seq, T_S, d),
            *[stack(outs_p, k, even) for k in range(5)], *[stack(outs_p, k, odd) for k in range(2)],
            *[stack(outs_s, k, even) for k in range(5)], *[stack(outs_s, k, odd) for k in range(2)])
```

```python
import functools
import math

import numpy as np
import jax
import jax.numpy as jnp
from jax import lax
from jax.experimental import pallas as pl
from jax.experimental.pallas import tpu as pltpu

F32 = jnp.float32
BF16 = jnp.bfloat16

D_MODEL = 2048
PAGE_SIZE = 128
Q_BLOCK = 128
H_A, KV_A, DQK_A = 8, 2, 64
G_A = H_A // KV_A
DV_A = 2 * DQK_A
H_B, KV_B, D_B = 8, 2, 128
G_B = H_B // KV_B
H_C, NOPE_C, ROPE_C, V_C = 16, 128, 64, 128
Q_LORA, KV_LORA = 512, 256
ROPE_THETA = 10000.0
N_GROUPS, EXP_PER_GROUP = 4, 8
N_EXPERTS = N_GROUPS * EXP_PER_GROUP
D_EXPERT = 512
DEPTH = 2
ALPHA = (2 * DEPTH) ** 0.25
DIFF_SCALE = DQK_A ** -0.5
FOX_SCALE = D_B ** -0.5
MLA_SCALE = (NOPE_C + ROPE_C) ** -0.5
LN_EPS = 1e-5
RMS_EPS = 1e-6

OFF_QD = 0
OFF_KD = H_A * 2 * DQK_A
OFF_VD = OFF_KD + KV_A * 2 * DQK_A
OFF_QF = OFF_VD + KV_A * DV_A
OFF_KF = OFF_QF + H_B * D_B
OFF_VF = OFF_KF + KV_B * D_B
OFF_FL = OFF_VF + KV_B * D_B
LANE = 128
VMEM_LIMIT = 56 * 2 ** 20

NT_DIMS = (((1,), (1,)), ((), ()))
NEG_INF = float("-inf")


def _cparams(sem):
    return pltpu.CompilerParams(dimension_semantics=sem, vmem_limit_bytes=VMEM_LIMIT)


def _dot(a, b):
    return jnp.dot(a, b, preferred_element_type=F32)


def _dot_nt(a, b):
    return lax.dot_general(a, b, NT_DIMS, preferred_element_type=F32)


def _split3(v):
    hi = v.astype(BF16)
    r1 = v - hi.astype(F32)
    mid = r1.astype(BF16)
    lo = (r1 - mid.astype(F32)).astype(BF16)
    return hi, mid, lo


def _mm_body(x_ref, w_ref, s_ref, o_ref, ob_ref, *, nt):
    x = x_ref[...].astype(BF16)
    w = w_ref[...].astype(BF16)
    acc = _dot_nt(x, w) if nt else _dot(x, w)
    o_ref[...] = acc
    ob_ref[...] = (acc * s_ref[...]).astype(BF16)


def matmul_scaled(x, w, col_scale, *, nt=False, n_out=None, tm=512, tn=512, name="matmul"):
    m, k = x.shape
    n = n_out if n_out is not None else (w.shape[0] if nt else w.shape[1])
    tm, tn = min(tm, m), min(tn, n)
    assert m % tm == 0 and n % tn == 0
    w_spec = pl.BlockSpec((tn, k), lambda i, j: (j, 0)) if nt else pl.BlockSpec((k, tn), lambda i, j: (0, j))
    out_spec = pl.BlockSpec((tm, tn), lambda i, j: (i, j))
    return pl.pallas_call(
        functools.partial(_mm_body, nt=nt),
        grid=(m // tm, n // tn),
        in_specs=[pl.BlockSpec((tm, k), lambda i, j: (i, 0)), w_spec, pl.BlockSpec((1, tn), lambda i, j: (0, j))],
        out_specs=[out_spec, out_spec],
        out_shape=[jax.ShapeDtypeStruct((m, n), F32), jax.ShapeDtypeStruct((m, n), BF16)],
        compiler_params=_cparams(("parallel", "parallel")),
        name=name,
    )(x, w, col_scale)


def _layer_norm(v, g, b):
    mu = jnp.mean(v, axis=-1, keepdims=True)
    d = v - mu
    var = jnp.mean(d * d, axis=-1, keepdims=True)
    return d * lax.rsqrt(var + LN_EPS) * g + b


def _mm_ln_body(x_ref, w_ref, r_ref, g_ref, b_ref, o_ref, acc_ref):
    k = pl.program_id(1)

    @pl.when(k == 0)
    def _():
        acc_ref[...] = jnp.zeros_like(acc_ref)

    acc_ref[...] += _dot(x_ref[...].astype(BF16), w_ref[...].astype(BF16))

    @pl.when(k == pl.num_programs(1) - 1)
    def _():
        o_ref[...] = _layer_norm(ALPHA * r_ref[...] + acc_ref[...], g_ref[...], b_ref[...])


def matmul_ln(x, w, resid, g, b, *, tm=512, tk=512, name="proj_ln"):
    m, kdim = x.shape
    n = w.shape[1]
    tm = min(tm, m)
    return pl.pallas_call(
        _mm_ln_body,
        grid=(m // tm, kdim // tk),
        in_specs=[pl.BlockSpec((tm, tk), lambda i, k: (i, k)), pl.BlockSpec((tk, n), lambda i, k: (k, 0)),
                  pl.BlockSpec((tm, n), lambda i, k: (i, 0)), pl.BlockSpec((1, n), lambda i, k: (0, 0)),
                  pl.BlockSpec((1, n), lambda i, k: (0, 0))],
        out_specs=pl.BlockSpec((tm, n), lambda i, k: (i, 0)),
        out_shape=jax.ShapeDtypeStruct((m, n), F32),
        scratch_shapes=[pltpu.VMEM((tm, n), F32)],
        compiler_params=_cparams(("parallel", "arbitrary")),
        name=name,
    )(x, w, resid, g, b)


def _logsig(z):
    return jnp.minimum(z, 0.0) - jnp.log1p(jnp.exp(-jnp.abs(z)))


def _forget_body(x_ref, wf_ref, wft_ref, b_ref, bt_ref, lf_ref, cum_ref, cumt_ref, carry_ref, carryt_ref, *, tm, seg):
    i = pl.program_id(0)

    @pl.when((i * tm) % seg == 0)
    def _():
        carry_ref[...] = jnp.zeros_like(carry_ref)
        carryt_ref[...] = jnp.zeros_like(carryt_ref)

    xb = x_ref[...].astype(BF16)
    lf = _logsig(_dot(xb, wf_ref[...].astype(BF16)) + b_ref[...])
    lft = _logsig(_dot_nt(wft_ref[...].astype(BF16), xb) + bt_ref[...])
    lf_ref[...] = lf
    r = lax.broadcasted_iota(jnp.int32, (tm, tm), 0)
    c = lax.broadcasted_iota(jnp.int32, (tm, tm), 1)
    same = (r // seg) == (c // seg) if seg < tm else (r >= 0)
    lower = jnp.where((c <= r) & same, 1.0, 0.0).astype(BF16)
    upper = jnp.where((r <= c) & same, 1.0, 0.0).astype(BF16)
    cum = carry_ref[0:1, :]
    for part in _split3(lf):
        cum = cum + _dot(lower, part)
    cumt = carryt_ref[:, 0:1]
    for part in _split3(lft):
        cumt = cumt + _dot(part, upper)
    cum_ref[...] = cum
    cumt_ref[...] = cumt
    carry_ref[...] = jnp.broadcast_to(cum[tm - 1:tm, :], carry_ref.shape)
    carryt_ref[...] = jnp.broadcast_to(cumt[:, tm - 1:tm], carryt_ref.shape)


def forget_gates(x, wf_pad, wft, b_row, b_col, *, seg, tm=256):
    m, k = x.shape
    tm = min(tm, m)
    assert m % tm == 0 and (seg % tm == 0 or tm % seg == 0)
    fixed = lambda i: (0, 0)
    return pl.pallas_call(
        functools.partial(_forget_body, tm=tm, seg=seg),
        grid=(m // tm,),
        in_specs=[pl.BlockSpec((tm, k), lambda i: (i, 0)), pl.BlockSpec((k, LANE), fixed), pl.BlockSpec((H_B, k), fixed),
                  pl.BlockSpec((1, LANE), fixed), pl.BlockSpec((H_B, 1), fixed)],
        out_specs=[pl.BlockSpec((tm, LANE), lambda i: (i, 0)), pl.BlockSpec((tm, LANE), lambda i: (i, 0)),
                   pl.BlockSpec((H_B, tm), lambda i: (0, i))],
        out_shape=[jax.ShapeDtypeStruct((m, LANE), F32), jax.ShapeDtypeStruct((m, LANE), F32),
                   jax.ShapeDtypeStruct((H_B, m), F32)],
        scratch_shapes=[pltpu.VMEM((8, LANE), F32), pltpu.VMEM((H_B, LANE), F32)],
        compiler_params=_cparams(("arbitrary",)),
        name="forget_gates",
    )(x, wf_pad, wft, b_row, b_col)


def _online_update(s, v_bf, m_ref, l_ref, acc_ref, idx=None):
    sel = (lambda r: r) if idx is None else (lambda r: r.at[idx])
    m_prev = sel(m_ref)[...]
    m_new = jnp.maximum(m_prev, jnp.max(s, axis=-1, keepdims=True))
    alpha = jnp.exp(m_prev - m_new)
    p = jnp.exp(s - m_new)
    sel(l_ref)[...] = alpha * sel(l_ref)[...] + jnp.sum(p, axis=-1, keepdims=True)
    sel(acc_ref)[...] = alpha * sel(acc_ref)[...] + _dot(p.astype(BF16), v_bf)
    sel(m_ref)[...] = m_new


def _softmax_step(s, v_bf, m_prev, acc_prev, l_prev=None):
    m_new = jnp.maximum(m_prev, jnp.max(s, axis=-1, keepdims=True))
    alpha = jnp.exp(m_prev - m_new)
    p = jnp.exp(s - m_new)
    acc_new = alpha * acc_prev + _dot(p.astype(BF16), v_bf)
    if l_prev is None:
        return m_new, acc_new
    return m_new, acc_new, alpha * l_prev + jnp.sum(p, axis=-1, keepdims=True)


def _with_ones(v_bf):
    return jnp.concatenate([v_bf, jnp.ones_like(v_bf)], axis=1)


def _diff_lambda(lq1, lk1, lq2, lk2, lam_init):
    s1 = jnp.sum(lq1 * lk1, axis=-1, keepdims=True)
    s2 = jnp.sum(lq2 * lk2, axis=-1, keepdims=True)
    return jnp.exp(s1) - jnp.exp(s2) + lam_init


def _subln(o, g, lam_init):
    ms = jnp.mean(o * o, axis=-1, keepdims=True)
    return o * lax.rsqrt(ms + LN_EPS) * g * (1.0 - lam_init)


ROW_SPLIT = 4
FOX_SPLIT = 4
MLA_SPLIT = 4


def _causal_tile(t):
    return lax.broadcasted_iota(jnp.int32, (t, t), 0) >= lax.broadcasted_iota(jnp.int32, (t, t), 1)


def _pair_tables(n_tiles):
    qi = np.array([i for i in range(n_tiles) for _ in range(i + 1)], np.int32)
    kj = np.array([j for i in range(n_tiles) for j in range(i + 1)], np.int32)
    return jnp.asarray(qi), jnp.asarray(kj)


def _diff_prompt_body(qi_ref, kj_ref, sl_ref, q_ref, k_ref, v_ref, lq1, lk1, lq2, lk2, g_ref, o_ref,
                      m_ref, acc_ref, *, t, lam_init):
    h = pl.program_id(0)
    n = pl.program_id(1)
    i = qi_ref[n]
    j = kj_ref[n]

    @pl.when(j == 0)
    def _():
        m_ref[...] = jnp.full_like(m_ref, NEG_INF)
        acc_ref[...] = jnp.zeros_like(acc_ref)

    q = q_ref[...]
    lane = lax.broadcasted_iota(jnp.int32, q.shape, 1)
    kb = k_ref[...]
    v1 = _with_ones(v_ref[...])
    kcol = lax.broadcasted_iota(jnp.int32, (1, t), 1)
    bias = sl_ref[h] * ((j - i) * t + kcol).astype(F32)

    def update(masked):
        sub = t // ROW_SPLIT
        res = []
        for c in range(2):
            qc = jnp.where((lane >= c * DQK_A) & (lane < (c + 1) * DQK_A), q, jnp.zeros_like(q))
            for r in range(ROW_SPLIT):
                rows = slice(r * sub, (r + 1) * sub)
                sc = _dot_nt(qc[rows], kb) + bias
                if masked:
                    sc = jnp.where(_causal_tile(t)[rows], sc, NEG_INF)
                res.append((c, rows, _softmax_step(sc, v1, m_ref[c, rows], acc_ref[c, rows])))
        for c, rows, (m_new, acc_new) in res:
            m_ref[c, rows] = m_new
            acc_ref[c, rows] = acc_new

    @pl.when(j < i)
    def _():
        update(False)

    @pl.when(j == i)
    def _():
        update(True)
        lam = _diff_lambda(lq1[...], lk1[...], lq2[...], lk2[...], lam_init)
        a0, a1 = acc_ref[0], acc_ref[1]
        o = a0[:, :DV_A] / a0[:, DV_A:DV_A + 1] - lam * (a1[:, :DV_A] / a1[:, DV_A:DV_A + 1])
        o_ref[...] = _subln(o, g_ref[...], lam_init)


def diff_prompt(h, slopes, lq1, lk1, lq2, lk2, g_subln, lam_init, *, t=1024):
    s = h.shape[0]
    t = min(t, s)
    qi, kj = _pair_tables(s // t)
    cb = lambda off: off // LANE
    small = lambda hh, n, qi, kj, sl: (0, 0)
    return pl.pallas_call(
        functools.partial(_diff_prompt_body, t=t, lam_init=lam_init),
        grid_spec=pltpu.PrefetchScalarGridSpec(
            num_scalar_prefetch=3,
            grid=(H_A, qi.shape[0]),
            in_specs=[pl.BlockSpec((t, LANE), lambda hh, n, qi, kj, sl: (qi[n], cb(OFF_QD) + hh)),
                      pl.BlockSpec((t, LANE), lambda hh, n, qi, kj, sl: (kj[n], cb(OFF_KD) + hh // G_A)),
                      pl.BlockSpec((t, LANE), lambda hh, n, qi, kj, sl: (kj[n], cb(OFF_VD) + hh // G_A)),
                      pl.BlockSpec((1, DQK_A), small), pl.BlockSpec((1, DQK_A), small),
                      pl.BlockSpec((1, DQK_A), small), pl.BlockSpec((1, DQK_A), small),
                      pl.BlockSpec((1, DV_A), small)],
            out_specs=pl.BlockSpec((t, LANE), lambda hh, n, qi, kj, sl: (qi[n], hh)),
            scratch_shapes=[pltpu.VMEM((2, t, 1), F32), pltpu.VMEM((2, t, 2 * DV_A), F32)]),
        out_shape=jax.ShapeDtypeStruct((s, H_A * DV_A), F32),
        compiler_params=_cparams(("parallel", "arbitrary")),
        name="diff_prompt",
    )(qi, kj, slopes, h, h, h, lq1, lk1, lq2, lk2, g_subln)


def _fox_prompt_body(qi_ref, kj_ref, q_ref, k_ref, v_ref, cq_ref, ck_ref, o_ref, m_ref, acc_ref, *, t):
    h = pl.program_id(0)
    n = pl.program_id(1)
    i = qi_ref[n]
    j = kj_ref[n]

    @pl.when(j == 0)
    def _():
        m_ref[...] = jnp.full_like(m_ref, NEG_INF)
        acc_ref[...] = jnp.zeros_like(acc_ref)

    qb = q_ref[...]
    kb = k_ref[...]
    v1 = _with_ones(v_ref[...])
    lane = lax.broadcasted_iota(jnp.int32, (1, LANE), 1)
    base = jnp.sum(jnp.where(lane == h, cq_ref[0:1, :], 0.0), axis=-1, keepdims=True)
    bias = base - ck_ref[...]

    def update(masked):
        sub = t // FOX_SPLIT
        res = []
        for r in range(FOX_SPLIT):
            rows = slice(r * sub, (r + 1) * sub)
            sc = _dot_nt(qb[rows], kb) + bias
            if masked:
                sc = jnp.where(_causal_tile(t)[rows], sc, NEG_INF)
            res.append((rows, _softmax_step(sc, v1, m_ref[rows], acc_ref[rows])))
        for rows, (m_new, acc_new) in res:
            m_ref[rows] = m_new
            acc_ref[rows] = acc_new

    @pl.when(j < i)
    def _():
        update(False)

    @pl.when(j == i)
    def _():
        update(True)
        acc = acc_ref[...]
        o_ref[...] = acc[:, :D_B] / acc[:, D_B:D_B + 1]


def fox_prompt(h, cum, cumt3, *, t=1024):
    s = h.shape[0]
    t = min(t, s)
    qi, kj = _pair_tables(s // t)
    cb = lambda off: off // LANE
    return pl.pallas_call(
        functools.partial(_fox_prompt_body, t=t),
        grid_spec=pltpu.PrefetchScalarGridSpec(
            num_scalar_prefetch=2,
            grid=(H_B, qi.shape[0]),
            in_specs=[pl.BlockSpec((t, LANE), lambda hh, n, qi, kj: (qi[n], cb(OFF_QF) + hh)),
                      pl.BlockSpec((t, LANE), lambda hh, n, qi, kj: (kj[n], cb(OFF_KF) + hh // G_B)),
                      pl.BlockSpec((t, LANE), lambda hh, n, qi, kj: (kj[n], cb(OFF_VF) + hh // G_B)),
                      pl.BlockSpec((t, LANE), lambda hh, n, qi, kj: (qi[n], 0)),
                      pl.BlockSpec((None, 1, t), lambda hh, n, qi, kj: (hh, 0, kj[n]))],
            out_specs=pl.BlockSpec((t, LANE), lambda hh, n, qi, kj: (qi[n], hh)),
            scratch_shapes=[pltpu.VMEM((t, 1), F32), pltpu.VMEM((t, 2 * D_B), F32)]),
        out_shape=jax.ShapeDtypeStruct((s, H_B * D_B), F32),
        compiler_params=_cparams(("parallel", "arbitrary")),
        name="fox_prompt",
    )(qi, kj, h, h, h, cum, cumt3)


CHUNK_PAGES = 16
CHUNK = CHUNK_PAGES * PAGE_SIZE
T_S = 8


def _chunk_schedule(pt_ref, b, k, n_chunks, n_seq):
    c = n_chunks - 1 - k
    last = k == n_chunks - 1
    nb = jnp.where(last, b + 1, b)
    nc = jnp.where(last, n_chunks - 1, c - 1)
    return c, nb, nc, jnp.logical_or(jnp.logical_not(last), b + 1 < n_seq)


def _even_copies(pt_ref, caches, bufs, sem, b, c, slot):
    out = []
    for p in range(CHUNK_PAGES):
        page = pt_ref[b, c * CHUNK_PAGES + p]
        for a, (src, dst) in enumerate(zip(caches, bufs)):
            out.append(pltpu.make_async_copy(src.at[page], dst.at[slot, p], sem.at[slot, a]))
    return out


def _even_sample_body(pt_ref, hs_ref, cs_ref, cst_ref, lq1, lk1, lq2, lk2, g_ref,
                      dk_hbm, dv_hbm, fk_hbm, fv_hbm, lf_hbm, od_ref, of_ref,
                      dkbuf, dvbuf, fkbuf, fvbuf, lfbuf, sem, ktb, vdb, kfb, vfb,
                      md, ld, accd, mf, lff, accf, run_ref, *, n_chunks, n_seq, past_len, lam_init):
    b = pl.program_id(0)
    caches = (dk_hbm, dv_hbm, fk_hbm, fv_hbm, lf_hbm)
    bufs = (dkbuf, dvbuf, fkbuf, fvbuf, lfbuf)
    n_d, n_f = 2 * KV_A * G_A * T_S, KV_B * G_B * T_S

    @pl.when(b == 0)
    def _():
        for cp in _even_copies(pt_ref, caches, bufs, sem, 0, n_chunks - 1, 0):
            cp.start()

    hs = hs_ref[...]
    lane = lax.broadcasted_iota(jnp.int32, (T_S, LANE), 1)
    zero = jnp.zeros((T_S, LANE), F32)
    rows = []
    for kv in range(KV_A):
        for c in range(2):
            for g in range(G_A):
                qh = hs[:, OFF_QD + (kv * G_A + g) * LANE:OFF_QD + (kv * G_A + g + 1) * LANE] * DIFF_SCALE
                qh = jnp.where((lane >= c * DQK_A) & (lane < (c + 1) * DQK_A), qh, 0.0)
                rows.append(jnp.concatenate([qh, zero] if kv == 0 else [zero, qh], axis=1))
    qd = jnp.concatenate(rows, axis=0).astype(BF16)
    rows = []
    for hh in range(H_B):
        qh = hs[:, OFF_QF + hh * LANE:OFF_QF + (hh + 1) * LANE] * FOX_SCALE
        rows.append(jnp.concatenate([qh, zero] if hh // G_B == 0 else [zero, qh], axis=1))
    qf = jnp.concatenate(rows, axis=0).astype(BF16)

    rd = lax.broadcasted_iota(jnp.int32, (n_d, 1), 0)
    head_d = (rd // (2 * G_A * T_S)) * G_A + (rd // T_S) % G_A
    slope = jnp.exp2(-(head_d + 1).astype(F32))
    t_d = rd % T_S
    qpos_d = (past_len + t_d).astype(F32)
    rf = lax.broadcasted_iota(jnp.int32, (n_f, 1), 0)
    t_f = rf % T_S
    cs = cs_ref[...]
    cq = jnp.concatenate([cs[:, hh:hh + 1] for hh in range(H_B)], axis=0)

    md[...] = jnp.full_like(md, NEG_INF)
    ld[...] = jnp.zeros_like(ld)
    accd[...] = jnp.zeros_like(accd)
    mf[...] = jnp.full_like(mf, NEG_INF)
    lff[...] = jnp.zeros_like(lff)
    accf[...] = jnp.zeros_like(accf)
    run_ref[...] = jnp.zeros_like(run_ref)

    pad = jnp.zeros((LANE - T_S, 2 * LANE), F32)
    key = lax.broadcasted_iota(jnp.int32, (1, LANE), 1)
    kd_new = jnp.concatenate([hs[:, OFF_KD:OFF_KD + 2 * LANE], pad], axis=0).astype(BF16)
    vd_new = jnp.concatenate([hs[:, OFF_VD:OFF_VD + 2 * LANE], pad], axis=0).astype(BF16)
    kf_new = jnp.concatenate([hs[:, OFF_KF:OFF_KF + 2 * LANE], pad], axis=0).astype(BF16)
    vf_new = jnp.concatenate([hs[:, OFF_VF:OFF_VF + 2 * LANE], pad], axis=0).astype(BF16)
    s = _dot_nt(qd, kd_new) - slope * (t_d - key).astype(F32)
    _online_update(jnp.where(key <= t_d, s, NEG_INF), vd_new, md, ld, accd)
    cst = cst_ref[...]
    cst_rows = jnp.broadcast_to(cst[:, None, :], (H_B, T_S, LANE)).reshape(n_f, LANE)
    s = _dot_nt(qf, kf_new) + (cq - cst_rows)
    _online_update(jnp.where(key <= t_f, s, NEG_INF), vf_new, mf, lff, accf)

    jj = lax.broadcasted_iota(jnp.int32, (PAGE_SIZE, PAGE_SIZE), 0)
    ss = lax.broadcasted_iota(jnp.int32, (PAGE_SIZE, PAGE_SIZE), 1)
    later = jnp.where(jj > ss, 1.0, 0.0).astype(BF16)
    kidx = lax.broadcasted_iota(jnp.int32, (1, CHUNK), 1)

    def step(k, carry):
        c, nb, nc, has_next = _chunk_schedule(pt_ref, b, k, n_chunks, n_seq)
        slot = k % 2

        @pl.when(has_next)
        def _():
            for cp in _even_copies(pt_ref, caches, bufs, sem, nb, nc, 1 - slot):
                cp.start()

        for cp in _even_copies(pt_ref, caches, bufs, sem, b, c, slot):
            cp.wait()

        for p in range(CHUNK_PAGES):
            lo, hi = p * PAGE_SIZE, (p + 1) * PAGE_SIZE
            ktb[:, lo:hi] = dkbuf[slot, p].astype(BF16)
            for kv in range(2):
                sl = pl.ds(kv, PAGE_SIZE, stride=2)
                vdb[lo:hi, kv * LANE:(kv + 1) * LANE] = dvbuf[slot, p, sl, :].astype(BF16)
                kfb[lo:hi, kv * LANE:(kv + 1) * LANE] = fkbuf[slot, p, sl, :].astype(BF16)
                vfb[lo:hi, kv * LANE:(kv + 1) * LANE] = fvbuf[slot, p, sl, :].astype(BF16)

        kpos = (c * CHUNK + kidx).astype(F32)
        s = _dot(qd, ktb[...]) - slope * (qpos_d - kpos)
        _online_update(s, vdb[...], md, ld, accd)

        x = lfbuf[slot].reshape(CHUNK_PAGES * H_B, PAGE_SIZE)
        within = jnp.zeros_like(x)
        for part in _split3(x):
            within = within + _dot(part, later)
        tot = jnp.sum(x, axis=-1, keepdims=True)
        run = run_ref[:, 0:1]
        pieces = [None] * CHUNK_PAGES
        for p in reversed(range(CHUNK_PAGES)):
            after = within[p * H_B:(p + 1) * H_B] + run
            pieces[p] = jnp.broadcast_to(after[:, None, :], (H_B, T_S, PAGE_SIZE)).reshape(n_f, PAGE_SIZE)
            run = run + tot[p * H_B:(p + 1) * H_B]
        run_ref[...] = jnp.broadcast_to(run, run_ref.shape)
        s = _dot_nt(qf, kfb[...]) + (cq + jnp.concatenate(pieces, axis=1))
        _online_update(s, vfb[...], mf, lff, accf)
        return carry

    lax.fori_loop(0, n_chunks, step, 0)

    lam = _diff_lambda(lq1[...], lk1[...], lq2[...], lk2[...], lam_init)
    od = accd[...] / ld[...]
    g = g_ref[...]
    for kv in range(KV_A):
        for gg in range(G_A):
            r0 = ((kv * 2 + 0) * G_A + gg) * T_S
            r1 = ((kv * 2 + 1) * G_A + gg) * T_S
            o = od[r0:r0 + T_S, kv * LANE:(kv + 1) * LANE] - lam * od[r1:r1 + T_S, kv * LANE:(kv + 1) * LANE]
            od_ref[:, (kv * G_A + gg) * LANE:(kv * G_A + gg + 1) * LANE] = _subln(o, g, lam_init)
    of = accf[...] / lff[...]
    for hh in range(H_B):
        kv = hh // G_B
        of_ref[:, hh * LANE:(hh + 1) * LANE] = of[hh * T_S:(hh + 1) * T_S, kv * LANE:(kv + 1) * LANE]


def even_sample(page_table, hs, cs, cst, lq1, lk1, lq2, lk2, g_subln, dk_t, dv2, fk2, fv2, lf_t, lam_init):
    n_seq, n_pages = page_table.shape
    assert n_pages % (2 * CHUNK_PAGES) == 0 and hs.shape[0] == n_seq * T_S
    n_chunks = n_pages // CHUNK_PAGES
    width = hs.shape[1]
    small = lambda b, pt: (0, 0)
    anyspec = pl.BlockSpec(memory_space=pl.ANY)
    page_buf = lambda: pltpu.VMEM((2, CHUNK_PAGES, 2 * LANE, PAGE_SIZE), F32)
    n_d, n_f = 2 * KV_A * G_A * T_S, KV_B * G_B * T_S
    return pl.pallas_call(
        functools.partial(_even_sample_body, n_chunks=n_chunks, n_seq=n_seq, past_len=n_pages * PAGE_SIZE, lam_init=lam_init),
        grid_spec=pltpu.PrefetchScalarGridSpec(
            num_scalar_prefetch=1,
            grid=(n_seq,),
            in_specs=[pl.BlockSpec((T_S, width), lambda b, pt: (b, 0)),
                      pl.BlockSpec((T_S, LANE), lambda b, pt: (b, 0)),
                      pl.BlockSpec((None, H_B, LANE), lambda b, pt: (b, 0, 0)),
                      pl.BlockSpec((1, DQK_A), small), pl.BlockSpec((1, DQK_A), small),
                      pl.BlockSpec((1, DQK_A), small), pl.BlockSpec((1, DQK_A), small),
                      pl.BlockSpec((1, DV_A), small),
                      anyspec, anyspec, anyspec, anyspec, anyspec],
            out_specs=[pl.BlockSpec((T_S, H_A * DV_A), lambda b, pt: (b, 0)),
                       pl.BlockSpec((T_S, H_B * D_B), lambda b, pt: (b, 0))],
            scratch_shapes=[page_buf(), page_buf(), page_buf(), page_buf(),
                            pltpu.VMEM((2, CHUNK_PAGES, H_B, PAGE_SIZE), F32),
                            pltpu.SemaphoreType.DMA((2, 5)),
                            pltpu.VMEM((2 * LANE, CHUNK), BF16), pltpu.VMEM((CHUNK, 2 * LANE), BF16),
                            pltpu.VMEM((CHUNK, 2 * LANE), BF16), pltpu.VMEM((CHUNK, 2 * LANE), BF16),
                            pltpu.VMEM((n_d, 1), F32), pltpu.VMEM((n_d, 1), F32), pltpu.VMEM((n_d, 2 * LANE), F32),
                            pltpu.VMEM((n_f, 1), F32), pltpu.VMEM((n_f, 1), F32), pltpu.VMEM((n_f, 2 * LANE), F32),
                            pltpu.VMEM((H_B, LANE), F32)]),
        out_shape=[jax.ShapeDtypeStruct((n_seq * T_S, H_A * DV_A), F32), jax.ShapeDtypeStruct((n_seq * T_S, H_B * D_B), F32)],
        compiler_params=_cparams(("arbitrary",)),
        name="even_sample",
    )(page_table, hs, cs, cst, lq1, lk1, lq2, lk2, g_subln, dk_t, dv2, fk2, fv2, lf_t)


MLA_CHUNK_PAGES = 32
MLA_CHUNK = MLA_CHUNK_PAGES * PAGE_SIZE


def _mla_copies(pt_ref, caches, bufs, sem, b, c, slot):
    out = []
    for p in range(MLA_CHUNK_PAGES):
        page = pt_ref[b, c * MLA_CHUNK_PAGES + p]
        for a, (src, dst) in enumerate(zip(caches, bufs)):
            out.append(pltpu.make_async_copy(src.at[page], dst.at[slot, p], sem.at[slot, a]))
    return out


def _mla_sample_body(pt_ref, q_ref, ckvn_ref, kpen_ref, ckv_hbm, kpe_hbm, o_ref,
                     ckvbuf, kpebuf, sem, ckvb, kpeb, m_ref, l_ref, acc_ref, *, n_chunks, n_seq):
    b = pl.program_id(0)
    caches = (ckv_hbm, kpe_hbm)
    bufs = (ckvbuf, kpebuf)
    n_r = H_C * T_S

    @pl.when(b == 0)
    def _():
        for cp in _mla_copies(pt_ref, caches, bufs, sem, 0, n_chunks - 1, 0):
            cp.start()

    q = q_ref[...].reshape(n_r, KV_LORA + ROPE_C) * MLA_SCALE
    ql = q[:, :KV_LORA].astype(BF16)
    qp = q[:, KV_LORA:].astype(BF16)
    t_r = lax.broadcasted_iota(jnp.int32, (n_r, 1), 0) % T_S
    key = lax.broadcasted_iota(jnp.int32, (1, LANE), 1)

    m_ref[...] = jnp.full_like(m_ref, NEG_INF)
    l_ref[...] = jnp.zeros_like(l_ref)
    acc_ref[...] = jnp.zeros_like(acc_ref)

    ckv_new = jnp.concatenate([ckvn_ref[...], jnp.zeros((LANE - T_S, KV_LORA), F32)], axis=0).astype(BF16)
    kpe_new = jnp.concatenate([kpen_ref[...], jnp.zeros((LANE - T_S, ROPE_C), F32)], axis=0).astype(BF16)
    s = _dot_nt(ql, ckv_new) + _dot_nt(qp, kpe_new)
    _online_update(jnp.where(key <= t_r, s, NEG_INF), ckv_new, m_ref, l_ref, acc_ref)

    def step(k, carry):
        c, nb, nc, has_next = _chunk_schedule(pt_ref, b, k, n_chunks, n_seq)
        slot = k % 2

        @pl.when(has_next)
        def _():
            for cp in _mla_copies(pt_ref, caches, bufs, sem, nb, nc, 1 - slot):
                cp.start()

        for cp in _mla_copies(pt_ref, caches, bufs, sem, b, c, slot):
            cp.wait()

        for p in range(MLA_CHUNK_PAGES):
            lo, hi = p * PAGE_SIZE, (p + 1) * PAGE_SIZE
            ckvb[lo:hi, :] = ckvbuf[slot, p].astype(BF16)
            kpeb[:, lo:hi] = kpebuf[slot, p].astype(BF16)

        half = MLA_CHUNK // 2
        parts = []
        for hlf in range(2):
            ks = slice(hlf * half, (hlf + 1) * half)
            kv = ckvb[ks, :]
            s = _dot_nt(ql, kv) + _dot(qp, kpeb[:, ks])
            m_h = jnp.max(s, axis=-1, keepdims=True)
            p = jnp.exp(s - m_h)
            parts.append((m_h, jnp.sum(p, axis=-1, keepdims=True), _dot(p.astype(BF16), kv)))
        m_prev = m_ref[...]
        m_new = jnp.maximum(m_prev, jnp.maximum(parts[0][0], parts[1][0]))
        w_prev = jnp.exp(m_prev - m_new)
        l_new = w_prev * l_ref[...]
        acc_new = w_prev * acc_ref[...]
        for m_h, l_h, acc_h in parts:
            w_h = jnp.exp(m_h - m_new)
            l_new = l_new + w_h * l_h
            acc_new = acc_new + w_h * acc_h
        m_ref[...] = m_new
        l_ref[...] = l_new
        acc_ref[...] = acc_new
        return carry

    lax.fori_loop(0, n_chunks, step, 0)
    o_ref[...] = (acc_ref[...] / l_ref[...]).reshape(H_C, T_S, KV_LORA)


def mla_sample(page_table, q3, ckv_new, kpe_new, ckv_pages, kpe_t):
    n_seq, n_pages = page_table.shape
    assert n_pages % (2 * MLA_CHUNK_PAGES) == 0
    n_chunks = n_pages // MLA_CHUNK_PAGES
    anyspec = pl.BlockSpec(memory_space=pl.ANY)
    n_r = H_C * T_S
    return pl.pallas_call(
        functools.partial(_mla_sample_body, n_chunks=n_chunks, n_seq=n_seq),
        grid_spec=pltpu.PrefetchScalarGridSpec(
            num_scalar_prefetch=1,
            grid=(n_seq,),
            in_specs=[pl.BlockSpec((H_C, T_S, KV_LORA + ROPE_C), lambda b, pt: (0, b, 0)),
                      pl.BlockSpec((T_S, KV_LORA), lambda b, pt: (b, 0)),
                      pl.BlockSpec((T_S, ROPE_C), lambda b, pt: (b, 0)),
                      anyspec, anyspec],
            out_specs=pl.BlockSpec((H_C, T_S, KV_LORA), lambda b, pt: (0, b, 0)),
            scratch_shapes=[pltpu.VMEM((2, MLA_CHUNK_PAGES, PAGE_SIZE, KV_LORA), F32),
                            pltpu.VMEM((2, MLA_CHUNK_PAGES, ROPE_C, PAGE_SIZE), F32),
                            pltpu.SemaphoreType.DMA((2, 2)),
                            pltpu.VMEM((MLA_CHUNK, KV_LORA), BF16), pltpu.VMEM((ROPE_C, MLA_CHUNK), BF16),
                            pltpu.VMEM((n_r, 1), F32), pltpu.VMEM((n_r, 1), F32), pltpu.VMEM((n_r, KV_LORA), F32)]),
        out_shape=jax.ShapeDtypeStruct((H_C, n_seq * T_S, KV_LORA), F32),
        compiler_params=_cparams(("arbitrary",)),
        name="mla_sample",
    )(page_table, q3, ckv_new, kpe_new, ckv_pages, kpe_t)


def _rope(x, cos2, sin2):
    half = x.shape[-1] // 2
    swapped = jnp.concatenate([x[:, half:], x[:, :half]], axis=1)
    return x * cos2 + swapped * sin2


def _rms(v, g, eps):
    return v * lax.rsqrt(jnp.mean(v * v, axis=-1, keepdims=True) + eps) * g


def _mla_cq_body(x_ref, w_ref, g_ref, o_ref):
    cq = _dot(x_ref[...].astype(BF16), w_ref[...].astype(BF16))
    o_ref[...] = _rms(cq, g_ref[...], RMS_EPS).astype(o_ref.dtype)


def mla_cq(x, w_dq, g_qn, *, tm=512):
    m, k = x.shape
    tm = min(tm, m)
    return pl.pallas_call(
        _mla_cq_body,
        grid=(m // tm,),
        in_specs=[pl.BlockSpec((tm, k), lambda i: (i, 0)), pl.BlockSpec((k, Q_LORA), lambda i: (0, 0)),
                  pl.BlockSpec((1, Q_LORA), lambda i: (0, 0))],
        out_specs=pl.BlockSpec((tm, Q_LORA), lambda i: (i, 0)),
        out_shape=jax.ShapeDtypeStruct((m, Q_LORA), BF16),
        compiler_params=_cparams(("parallel",)),
        name="mla_cq",
    )(x, w_dq, g_qn)


def _mla_q_body(cq_ref, wuq_ref, wuk_ref, cos_ref, sin_ref, o_ref, *, scale, absorb):
    q = _dot_nt(cq_ref[...].astype(BF16), wuq_ref[...].astype(BF16))
    q_pe = (_rope(q[:, NOPE_C:], cos_ref[...], sin_ref[...]) * scale).astype(o_ref.dtype)
    if absorb:
        q_lat = _dot_nt(q[:, :NOPE_C].astype(BF16), wuk_ref[...].astype(BF16))
        o_ref[:, :KV_LORA] = (q_lat * scale).astype(o_ref.dtype)
        o_ref[:, KV_LORA:] = q_pe
    else:
        o_ref[:, :NOPE_C] = (q[:, :NOPE_C] * scale).astype(o_ref.dtype)
        o_ref[:, NOPE_C:] = q_pe


def mla_q(cq, wuq_t, w_ukv2, cos2, sin2, *, scale=1.0, out_dtype=F32, absorb=True, tm=512):
    m = cq.shape[0]
    tm = min(tm, m)
    width = (KV_LORA if absorb else NOPE_C) + ROPE_C
    return pl.pallas_call(
        functools.partial(_mla_q_body, scale=scale, absorb=absorb),
        grid=(H_C, m // tm),
        in_specs=[pl.BlockSpec((tm, Q_LORA), lambda h, i: (i, 0)),
                  pl.BlockSpec((None, NOPE_C + ROPE_C, Q_LORA), lambda h, i: (h, 0, 0)),
                  pl.BlockSpec((KV_LORA, NOPE_C), lambda h, i: (0, 2 * h)),
                  pl.BlockSpec((tm, ROPE_C), lambda h, i: (i, 0)), pl.BlockSpec((tm, ROPE_C), lambda h, i: (i, 0))],
        out_specs=pl.BlockSpec((None, tm, width), lambda h, i: (h, i, 0)),
        out_shape=jax.ShapeDtypeStruct((H_C, m, width), out_dtype),
        compiler_params=_cparams(("parallel", "parallel")),
        name="mla_q",
    )(cq, wuq_t, w_ukv2, cos2, sin2)


def _mla_kv_body(x_ref, w_ref, g_ref, cos_ref, sin_ref, ckv_ref, kpe_ref, ckvb_ref, kpeb_ref):
    kv = _dot_nt(x_ref[...].astype(BF16), w_ref[...].astype(BF16))
    ckv = _rms(kv[:, :KV_LORA], g_ref[...], RMS_EPS)
    kpe = _rope(kv[:, KV_LORA:], cos_ref[...], sin_ref[...])
    ckv_ref[...] = ckv
    kpe_ref[...] = kpe
    ckvb_ref[...] = ckv.astype(BF16)
    kpeb_ref[...] = kpe.astype(BF16)


def mla_kv(x, wdkv_t, g_kvn, cos2, sin2, *, tm=512):
    m, k = x.shape
    tm = min(tm, m)
    return pl.pallas_call(
        _mla_kv_body,
        grid=(m // tm,),
        in_specs=[pl.BlockSpec((tm, k), lambda i: (i, 0)), pl.BlockSpec((KV_LORA + ROPE_C, k), lambda i: (0, 0)),
                  pl.BlockSpec((1, KV_LORA), lambda i: (0, 0)),
                  pl.BlockSpec((tm, ROPE_C), lambda i: (i, 0)), pl.BlockSpec((tm, ROPE_C), lambda i: (i, 0))],
        out_specs=[pl.BlockSpec((tm, KV_LORA), lambda i: (i, 0)), pl.BlockSpec((tm, ROPE_C), lambda i: (i, 0)),
                   pl.BlockSpec((tm, KV_LORA), lambda i: (i, 0)), pl.BlockSpec((tm, ROPE_C), lambda i: (i, 0))],
        out_shape=[jax.ShapeDtypeStruct((m, KV_LORA), F32), jax.ShapeDtypeStruct((m, ROPE_C), F32),
                   jax.ShapeDtypeStruct((m, KV_LORA), BF16), jax.ShapeDtypeStruct((m, ROPE_C), BF16)],
        compiler_params=_cparams(("parallel",)),
        name="mla_kv",
    )(x, wdkv_t, g_kvn, cos2, sin2)


def _mla_kv_up_body(ckv_ref, kpe_ref, w_ref, k_ref, v_ref):
    up = _dot(ckv_ref[...], w_ref[...].astype(BF16))
    k_ref[:, :NOPE_C] = up[:, :NOPE_C].astype(k_ref.dtype)
    k_ref[:, NOPE_C:] = kpe_ref[...]
    v_ref[...] = up[:, NOPE_C:].astype(v_ref.dtype)


def mla_kv_up(ckv_b, kpe_b, w_ukv2, *, tm=512):
    m = ckv_b.shape[0]
    tm = min(tm, m)
    return pl.pallas_call(
        _mla_kv_up_body,
        grid=(H_C, m // tm),
        in_specs=[pl.BlockSpec((tm, KV_LORA), lambda h, i: (i, 0)), pl.BlockSpec((tm, ROPE_C), lambda h, i: (i, 0)),
                  pl.BlockSpec((KV_LORA, NOPE_C + V_C), lambda h, i: (0, h))],
        out_specs=[pl.BlockSpec((None, tm, NOPE_C + ROPE_C), lambda h, i: (h, i, 0)),
                   pl.BlockSpec((None, tm, V_C), lambda h, i: (h, i, 0))],
        out_shape=[jax.ShapeDtypeStruct((H_C, m, NOPE_C + ROPE_C), ckv_b.dtype), jax.ShapeDtypeStruct((H_C, m, V_C), ckv_b.dtype)],
        compiler_params=_cparams(("parallel", "parallel")),
        name="mla_kv_up",
    )(ckv_b, kpe_b, w_ukv2)


def _mha_prompt_body(qi_ref, kj_ref, q_ref, k_ref, v_ref, o_ref, m_ref, acc_ref, *, t):
    n = pl.program_id(1)
    i = qi_ref[n]
    j = kj_ref[n]

    @pl.when(j == 0)
    def _():
        m_ref[...] = jnp.full_like(m_ref, NEG_INF)
        acc_ref[...] = jnp.zeros_like(acc_ref)

    qb = q_ref[...]
    kb = k_ref[...]
    v1 = _with_ones(v_ref[...])

    def update(masked):
        sub = t // MLA_SPLIT
        res = []
        for r in range(MLA_SPLIT):
            rows = slice(r * sub, (r + 1) * sub)
            sc = _dot_nt(qb[rows], kb)
            if masked:
                sc = jnp.where(_causal_tile(t)[rows], sc, NEG_INF)
            res.append((rows, _softmax_step(sc, v1, m_ref[rows], acc_ref[rows])))
        for rows, (m_new, acc_new) in res:
            m_ref[rows] = m_new
            acc_ref[rows] = acc_new

    @pl.when(j < i)
    def _():
        update(False)

    @pl.when(j == i)
    def _():
        update(True)
        acc = acc_ref[...]
        o_ref[...] = acc[:, :V_C] / acc[:, V_C:V_C + 1]


def mha_prompt(q3, k3, v3, *, t=1024):
    n_h, s, dq = q3.shape
    dv = v3.shape[2]
    t = min(t, s)
    qi, kj = _pair_tables(s // t)
    return pl.pallas_call(
        functools.partial(_mha_prompt_body, t=t),
        grid_spec=pltpu.PrefetchScalarGridSpec(
            num_scalar_prefetch=2,
            grid=(n_h, qi.shape[0]),
            in_specs=[pl.BlockSpec((None, t, dq), lambda hh, n, qi, kj: (hh, qi[n], 0)),
                      pl.BlockSpec((None, t, dq), lambda hh, n, qi, kj: (hh, kj[n], 0)),
                      pl.BlockSpec((None, t, dv), lambda hh, n, qi, kj: (hh, kj[n], 0))],
            out_specs=pl.BlockSpec((t, dv), lambda hh, n, qi, kj: (qi[n], hh)),
            scratch_shapes=[pltpu.VMEM((t, 1), F32), pltpu.VMEM((t, 2 * dv), F32)]),
        out_shape=jax.ShapeDtypeStruct((s, n_h * dv), F32),
        compiler_params=_cparams(("parallel", "arbitrary")),
        name="mha_prompt",
    )(qi, kj, q3, k3, v3)


def _mla_ov_body(o_ref, wuv_ref, out_ref):
    out_ref[...] = _dot(o_ref[...].astype(BF16), wuv_ref[...].astype(BF16))


def mla_ov(o3, w_ukv2, *, tm=512):
    m = o3.shape[1]
    tm = min(tm, m)
    return pl.pallas_call(
        _mla_ov_body,
        grid=(H_C, m // tm),
        in_specs=[pl.BlockSpec((None, tm, KV_LORA), lambda h, i: (h, i, 0)),
                  pl.BlockSpec((KV_LORA, V_C), lambda h, i: (0, 2 * h + 1))],
        out_specs=pl.BlockSpec((tm, V_C), lambda h, i: (i, h)),
        out_shape=jax.ShapeDtypeStruct((m, H_C * V_C), F32),
        compiler_params=_cparams(("parallel", "parallel")),
        name="mla_ov",
    )(o3, w_ukv2)


def _router_body(x_ref, w_ref, b_ref, gates_ref):
    x = x_ref[...]
    w = w_ref[...]
    xh = x.astype(BF16)
    xl = (x - xh.astype(F32)).astype(BF16)
    wh = w.astype(BF16)
    wl = (w - wh.astype(F32)).astype(BF16)
    logit = _dot(xh, wh) + _dot(xh, wl) + _dot(xl, wh) + b_ref[...]
    lane = lax.broadcasted_iota(jnp.int32, logit.shape, 1)
    big = jnp.int32(2 ** 30)

    def first_max(mask):
        v = jnp.max(jnp.where(mask, logit, NEG_INF), axis=-1, keepdims=True)
        idx = jnp.min(jnp.where(mask & (logit == v), lane, big), axis=-1, keepdims=True)
        return v, idx

    is_group = (lane >= N_EXPERTS) & (lane < N_EXPERTS + N_GROUPS)
    gmax, gidx = first_max(is_group)
    w_grp = 1.0 / jnp.sum(jnp.where(is_group, jnp.exp(logit - gmax), 0.0), axis=-1, keepdims=True)
    g_sel = gidx - N_EXPERTS
    in_grp = (lane >= g_sel * EXP_PER_GROUP) & (lane < (g_sel + 1) * EXP_PER_GROUP)
    v1, i1 = first_max(in_grp)
    v2, i2 = first_max(in_grp & (lane != i1))
    e2 = jnp.exp(v2 - v1)
    w1 = w_grp / (1.0 + e2)
    w2 = w_grp * e2 / (1.0 + e2)
    route = jnp.where(lane == 0, i1.astype(F32), 0.0) + jnp.where(lane == 1, i2.astype(F32), 0.0)
    gates_ref[...] = route + jnp.where(lane == 2, w1, 0.0) + jnp.where(lane == 3, w2, 0.0)


def router(x, w_pad, b_pad, *, tm=512):
    m, k = x.shape
    tm = min(tm, m)
    return pl.pallas_call(
        _router_body,
        grid=(m // tm,),
        in_specs=[pl.BlockSpec((tm, k), lambda i: (i, 0)), pl.BlockSpec((k, LANE), lambda i: (0, 0)),
                  pl.BlockSpec((1, LANE), lambda i: (0, 0))],
        out_specs=pl.BlockSpec((tm, LANE), lambda i: (i, 0)),
        out_shape=jax.ShapeDtypeStruct((m, LANE), F32),
        compiler_params=_cparams(("parallel",)),
        name="router",
    )(x, w_pad, b_pad)


MOE_TILE = 256
ROW_UNROLL = 8


def _row_gather(idx_smem, islot, src_hbm, dst, dslot, sem, n_rows, wait):
    def body(blk, carry):
        for k in range(ROW_UNROLL):
            r = blk * ROW_UNROLL + k
            row = idx_smem[islot, r]
            cp = pltpu.make_async_copy(src_hbm.at[pl.ds(row, 1), :], dst.at[dslot, pl.ds(r, 1), :], sem.at[dslot])
            cp.wait() if wait else cp.start(priority=k % 2)
        return carry
    lax.fori_loop(0, n_rows // ROW_UNROLL, body, 0)


def _gather_pipeline(t, n_act, idx_hbm, idx_smem, isem, src_hbm, dst, gsem, n_rows):
    idx_copy = lambda tile: pltpu.make_async_copy(idx_hbm.at[tile], idx_smem.at[tile % 3], isem.at[tile % 3])

    @pl.when(t == 0)
    def _():
        idx_copy(0).start()
        idx_copy(0).wait()
        _row_gather(idx_smem, 0, src_hbm, dst, 0, gsem, n_rows, wait=False)

        @pl.when(n_act > 1)
        def _():
            idx_copy(1).start()

    @pl.when(t + 1 < n_act)
    def _():
        idx_copy(t + 1).wait()
        _row_gather(idx_smem, (t + 1) % 3, src_hbm, dst, (t + 1) % 2, gsem, n_rows, wait=False)

        @pl.when(t + 2 < n_act)
        def _():
            idx_copy(t + 2).start()

    @pl.when(t < n_act)
    def _():
        _row_gather(idx_smem, t % 3, src_hbm, dst, t % 2, gsem, n_rows, wait=True)


def _moe_ffn_body(te_ref, na_ref, tok_hbm, x_hbm, gate_ref, wg_ref, wu_ref, wd_ref, o_ref,
                  xbuf, tok_smem, gsem, isem, wgb, wub, wdb):
    t = pl.program_id(0)
    n_act = na_ref[0]
    _gather_pipeline(t, n_act, tok_hbm, tok_smem, isem, x_hbm, xbuf, gsem, MOE_TILE)

    @pl.when(t < n_act)
    def _():
        @pl.when((t == 0) | (te_ref[t] != te_ref[jnp.maximum(t - 1, 0)]))
        def _():
            wgb[...] = wg_ref[...].astype(BF16)
            wub[...] = wu_ref[...].astype(BF16)
            wdb[...] = wd_ref[...].astype(BF16)

        xb = xbuf[t % 2].astype(BF16)
        a = _dot(xb, wgb[...])
        u = _dot(xb, wub[...])
        hmid = a * (1.0 / (1.0 + jnp.exp(-a))) * u * gate_ref[...]
        o_ref[...] = _dot(hmid.astype(BF16), wdb[...])

    @pl.when(t >= n_act)
    def _():
        o_ref[...] = jnp.zeros_like(o_ref)


def moe_ffn(tile_expert, n_active, tok_tiles, x, gate_col, w_gate, w_up, w_down, layer):
    n_tiles = tok_tiles.shape[0]
    d = x.shape[1]
    wspec = lambda shape: pl.BlockSpec((None, None) + shape, lambda t, te, na: (layer, te[t], 0, 0))
    return pl.pallas_call(
        _moe_ffn_body,
        grid_spec=pltpu.PrefetchScalarGridSpec(
            num_scalar_prefetch=2,
            grid=(n_tiles,),
            in_specs=[pl.BlockSpec(memory_space=pl.ANY), pl.BlockSpec(memory_space=pl.ANY),
                      pl.BlockSpec((MOE_TILE, 1), lambda t, te, na: (t, 0)),
                      wspec((d, D_EXPERT)), wspec((d, D_EXPERT)), wspec((D_EXPERT, d))],
            out_specs=pl.BlockSpec((MOE_TILE, d), lambda t, te, na: (t, 0)),
            scratch_shapes=[pltpu.VMEM((2, MOE_TILE, d), F32), pltpu.SMEM((3, MOE_TILE), jnp.int32),
                            pltpu.SemaphoreType.DMA((2,)), pltpu.SemaphoreType.DMA((3,)),
                            pltpu.VMEM((d, D_EXPERT), BF16), pltpu.VMEM((d, D_EXPERT), BF16),
                            pltpu.VMEM((D_EXPERT, d), BF16)]),
        out_shape=jax.ShapeDtypeStruct((n_tiles * MOE_TILE, d), F32),
        compiler_params=_cparams(("arbitrary",)),
        name="moe_ffn",
    )(tile_expert, n_active, tok_tiles, x, gate_col, w_gate, w_up, w_down)


def _moe_combine_body(pos_hbm, y_hbm, x_ref, g_ref, b_ref, o_ref, ybuf, pos_smem, gsem, isem, *, tm):
    t = pl.program_id(0)
    _gather_pipeline(t, pl.num_programs(0), pos_hbm, pos_smem, isem, y_hbm, ybuf, gsem, 2 * tm)
    y = ybuf[t % 2, :tm] + ybuf[t % 2, tm:]
    o_ref[...] = _layer_norm(ALPHA * x_ref[...] + y, g_ref[...], b_ref[...])


def moe_combine_ln(pos_tiles, y_sorted, x, g, b, *, tm):
    m, d = x.shape
    row = lambda i: (i, 0)
    fixed = lambda i: (0, 0)
    return pl.pallas_call(
        functools.partial(_moe_combine_body, tm=tm),
        grid=(m // tm,),
        in_specs=[pl.BlockSpec(memory_space=pl.ANY), pl.BlockSpec(memory_space=pl.ANY),
                  pl.BlockSpec((tm, d), row), pl.BlockSpec((1, d), fixed), pl.BlockSpec((1, d), fixed)],
        out_specs=pl.BlockSpec((tm, d), row),
        out_shape=jax.ShapeDtypeStruct((m, d), F32),
        scratch_shapes=[pltpu.VMEM((2, 2 * tm, d), F32), pltpu.SMEM((3, 2 * tm), jnp.int32),
                        pltpu.SemaphoreType.DMA((2,)), pltpu.SemaphoreType.DMA((3,))],
        compiler_params=_cparams(("arbitrary",)),
        name="moe_combine_ln",
    )(pos_tiles, y_sorted, x, g, b)


def _routing_tables(route, tm_tok):
    n = route.shape[0]
    n_asg = 2 * n
    n_tiles = (n_asg + N_EXPERTS * MOE_TILE) // MOE_TILE
    ef = route[:, 0:2].astype(jnp.int32).reshape(n_asg)
    wf = route[:, 2:4].reshape(n_asg)
    order = jnp.argsort(ef, stable=True).astype(jnp.int32)
    e_sorted = ef[order]
    bounds = jnp.searchsorted(e_sorted, jnp.arange(N_EXPERTS + 1, dtype=jnp.int32), side="left").astype(jnp.int32)
    counts = bounds[1:] - bounds[:-1]
    padded = ((counts + MOE_TILE - 1) // MOE_TILE) * MOE_TILE
    seg_end_pad = jnp.cumsum(padded)
    seg_start_pad = seg_end_pad - padded
    seg_start = jnp.cumsum(counts) - counts
    n_active = (seg_end_pad[-1] // MOE_TILE).astype(jnp.int32)
    tile_start = jnp.arange(n_tiles, dtype=jnp.int32) * MOE_TILE
    tile_expert = jnp.minimum(jnp.searchsorted(seg_end_pad, tile_start, side="right"), N_EXPERTS - 1).astype(jnp.int32)
    tile_expert = jnp.where(jnp.arange(n_tiles) < n_active, tile_expert, tile_expert[jnp.maximum(n_active - 1, 0)])
    p = jnp.arange(n_tiles * MOE_TILE, dtype=jnp.int32)
    e_p = jnp.repeat(tile_expert, MOE_TILE)
    local = p - seg_start_pad[e_p]
    valid = (local < counts[e_p]) & (p < seg_end_pad[-1])
    rank = jnp.clip(seg_start[e_p] + local, 0, n_asg - 1)
    asg = order[rank]
    row_token = jnp.where(valid, asg // 2, 0).astype(jnp.int32)
    row_gate = jnp.where(valid, wf[asg], 0.0)
    ppos = seg_start_pad[e_sorted] + (jnp.arange(n_asg, dtype=jnp.int32) - seg_start[e_sorted])
    pos_a = jnp.zeros((n_asg,), jnp.int32).at[order].set(ppos.astype(jnp.int32), unique_indices=True)
    pos_tiles = jnp.transpose(pos_a.reshape(n // tm_tok, tm_tok, 2), (0, 2, 1)).reshape(n // tm_tok, 2 * tm_tok)
    return (tile_expert, n_active.reshape(1), row_token.reshape(n_tiles, MOE_TILE),
            row_gate.reshape(n_tiles * MOE_TILE, 1), pos_tiles)


def _rope_tables(pos):
    half = ROPE_C // 2
    freqs = ROPE_THETA ** (-jnp.arange(half, dtype=F32) / half)
    ang = pos.astype(F32)[:, None] * freqs
    cos, sin = jnp.cos(ang), jnp.sin(ang)
    return jnp.concatenate([cos, cos], axis=1), jnp.concatenate([-sin, sin], axis=1)


def _lambda_init(layer_idx):
    return 0.8 - 0.6 * math.exp(-0.3 * layer_idx)


def _row(v):
    return v.reshape(1, -1)


def _moe_ln_block(x, l, g, b, w_router_group, b_router_group, w_router_expert, b_router_expert, w_gate, w_up, w_down):
    pad = jnp.zeros((D_MODEL, LANE - N_EXPERTS - N_GROUPS), F32)
    w_pad = jnp.concatenate([w_router_expert[l], w_router_group[l], pad], axis=1)
    b_pad = jnp.concatenate([b_router_expert[l], b_router_group[l], jnp.zeros((LANE - N_EXPERTS - N_GROUPS,), F32)])[None]
    tm_tok = math.gcd(x.shape[0], MOE_TILE)
    route = router(x, w_pad, b_pad, tm=tm_tok)
    tile_expert, n_active, tok_tiles, gate_col, pos_tiles = _routing_tables(route, tm_tok)
    y_sorted = moe_ffn(tile_expert, n_active, tok_tiles, x, gate_col, w_gate, w_up, w_down, l)
    return moe_combine_ln(pos_tiles, y_sorted, x, g, b, tm=tm_tok)


def kernel(x_prompt, x_sample, cache_diff_k, cache_diff_v, cache_fox_k, cache_fox_v, cache_fox_logf, cache_mla_ckv, cache_mla_kpe, page_table, w_in_even, b_forget, lambda_q1, lambda_k1, lambda_q2, lambda_k2, g_subln, w_out_even, w_dq, g_q_norm, w_uq, w_dkv, g_kv_norm, w_ukv, w_o_mla, ln_attn_g, ln_attn_b, ln_ffn_g, ln_ffn_b, w_router_group, b_router_group, w_router_expert, b_router_expert, w_gate, w_up, w_down):
    n_b, s_len, d = x_prompt.shape
    n_seq, t_new, _ = x_sample.shape
    assert n_b == 1 and t_new == T_S
    n_pool = cache_diff_k.shape[1]
    past = page_table.shape[1] * PAGE_SIZE
    xp = x_prompt.reshape(s_len, d)
    xs = x_sample.reshape(n_seq * T_S, d)
    slopes = 2.0 ** (-8.0 * jnp.arange(1, H_A + 1, dtype=F32) / H_A)
    col = np.arange(OFF_FL)
    col_scale = jnp.asarray(np.where(col < OFF_KD, DIFF_SCALE, np.where((col >= OFF_QF) & (col < OFF_KF), FOX_SCALE, 1.0)),
                            F32).reshape(1, OFF_FL)
    moe_w = (w_router_group, b_router_group, w_router_expert, b_router_expert, w_gate, w_up, w_down)
    outs_p, outs_s = {}, {}

    for l in range(DEPTH):
        i = l // 2
        if l % 2 == 0:
            lam_init = _lambda_init(l)
            w_t = jnp.transpose(w_in_even[i])
            wft = w_t[OFF_FL:OFF_FL + H_B]
            wf_pad = jnp.concatenate([jnp.transpose(wft), jnp.zeros((d, LANE - H_B), F32)], axis=1)
            b_row = jnp.concatenate([b_forget[i], jnp.zeros((LANE - H_B,), F32)])[None]
            b_col = b_forget[i][:, None]
            lam_args = (_row(lambda_q1[i]), _row(lambda_k1[i]), _row(lambda_q2[i]), _row(lambda_k2[i]), _row(g_subln[i]))
            hp, hp_b = matmul_scaled(xp, w_t, col_scale, nt=True, n_out=OFF_FL, name="even_proj_p")
            lf_p, cum_p, cumt_p = forget_gates(xp, wf_pad, wft, b_row, b_col, seg=s_len)
            od = diff_prompt(hp_b, slopes, *lam_args, lam_init)
            of = fox_prompt(hp_b, cum_p, cumt_p.reshape(H_B, 1, s_len))
            xp = matmul_ln(jnp.concatenate([od, of], axis=1), w_out_even[i], xp,
                           _row(ln_attn_g[l]), _row(ln_attn_b[l]), name="even_out_p")
            outs_p[l] = (hp[:, OFF_KD:OFF_VD].reshape(1, s_len, KV_A, 2, DQK_A), hp[:, OFF_VD:OFF_QF].reshape(1, s_len, KV_A, DV_A),
                         hp[:, OFF_KF:OFF_VF].reshape(1, s_len, KV_B, D_B), hp[:, OFF_VF:OFF_FL].reshape(1, s_len, KV_B, D_B),
                         lf_p[:, :H_B].reshape(1, s_len, H_B))
            hs, _ = matmul_scaled(xs, w_t, col_scale, nt=True, n_out=OFF_FL, name="even_proj_s")
            lf_s, cum_s, cumt_s = forget_gates(xs, wf_pad, wft, b_row, b_col, seg=T_S)
            cst = jnp.transpose(cumt_s.reshape(H_B, n_seq, T_S), (1, 0, 2))
            cst = jnp.concatenate([cst, jnp.zeros((n_seq, H_B, LANE - T_S), F32)], axis=2)
            dk_t = jnp.transpose(cache_diff_k[i], (0, 2, 3, 4, 1)).reshape(n_pool, KV_A * 2 * DQK_A, PAGE_SIZE)
            dv2 = cache_diff_v[i].reshape(n_pool, PAGE_SIZE * KV_A, DV_A)
            fk2 = cache_fox_k[i].reshape(n_pool, PAGE_SIZE * KV_B, D_B)
            fv2 = cache_fox_v[i].reshape(n_pool, PAGE_SIZE * KV_B, D_B)
            lf_t = jnp.transpose(cache_fox_logf[i], (0, 2, 1))
            od, of = even_sample(page_table, hs, cum_s, cst, *lam_args, dk_t, dv2, fk2, fv2, lf_t, lam_init)
            xs = matmul_ln(jnp.concatenate([od, of], axis=1), w_out_even[i], xs,
                           _row(ln_attn_g[l]), _row(ln_attn_b[l]), name="even_out_s")
            outs_s[l] = (hs[:, OFF_KD:OFF_VD].reshape(n_seq, T_S, KV_A, 2, DQK_A), hs[:, OFF_VD:OFF_QF].reshape(n_seq, T_S, KV_A, DV_A),
                         hs[:, OFF_KF:OFF_VF].reshape(n_seq, T_S, KV_B, D_B), hs[:, OFF_VF:OFF_FL].reshape(n_seq, T_S, KV_B, D_B),
                         lf_s[:, :H_B].reshape(n_seq, T_S, H_B))
        else:
            wuq_t = jnp.transpose(w_uq[i], (1, 2, 0))
            wdkv_t = jnp.transpose(w_dkv[i])
            w_ukv2 = w_ukv[i].reshape(KV_LORA, H_C * (NOPE_C + V_C))
            kpe_t = jnp.transpose(cache_mla_kpe[i], (0, 2, 1))
            for grp in ("p", "s"):
                x = xp if grp == "p" else xs
                pos = jnp.arange(s_len, dtype=jnp.int32) if grp == "p" else past + jnp.tile(jnp.arange(T_S, dtype=jnp.int32), n_seq)
                cos2, sin2 = _rope_tables(pos)
                cq = mla_cq(x, w_dq[i], _row(g_q_norm[i]))
                ckv, kpe, ckv_b, kpe_b = mla_kv(x, wdkv_t, _row(g_kv_norm[i]), cos2, sin2)
                if grp == "p":
                    q3 = mla_q(cq, wuq_t, w_ukv2, cos2, sin2, scale=MLA_SCALE, out_dtype=BF16, absorb=False)
                    k3, v3 = mla_kv_up(ckv_b, kpe_b, w_ukv2)
                    ov = mha_prompt(q3, k3, v3)
                else:
                    q3 = mla_q(cq, wuq_t, w_ukv2, cos2, sin2)
                    o3 = mla_sample(page_table, q3, ckv, kpe, cache_mla_ckv[i], kpe_t)
                    ov = mla_ov(o3, w_ukv2)
                x = matmul_ln(ov, w_o_mla[i], x, _row(ln_attn_g[l]), _row(ln_attn_b[l]), name="mla_out_" + grp)
                if grp == "p":
                    xp = x
                    outs_p[l] = (ckv.reshape(1, s_len, KV_LORA), kpe.reshape(1, s_len, ROPE_C))
                else:
                    xs = x
                    outs_s[l] = (ckv.reshape(n_seq, T_S, KV_LORA), kpe.reshape(n_seq, T_S, ROPE_C))
        x_all = _moe_ln_block(jnp.concatenate([xp, xs], axis=0), l, _row(ln_ffn_g[l]), _row(ln_ffn_b[l]), *moe_w)
        xp, xs = x_all[:s_len], x_all[s_len:]

    stack = lambda outs, k, ls: jnp.stack([outs[l][k] for l in ls])
    even, odd = range(0, DEPTH, 2), range(1, DEPTH, 2)
    return (xp.reshape(1, s_len, d), xs.reshape(n_seq, T_S, d),
            *[stack(outs_p, k, even) for k in range(5)], *[stack(outs_p, k, odd) for k in range(2)],
            *[stack(outs_s, k, even) for k in range(5)], *[stack(outs_s, k, odd) for k in range(2)])
```

```python
import functools
import math

import numpy as np
import jax
import jax.numpy as jnp
from jax import lax
from jax.experimental import pallas as pl
from jax.experimental.pallas import tpu as pltpu

F32 = jnp.float32
BF16 = jnp.bfloat16

D_MODEL = 2048
PAGE_SIZE = 128
Q_BLOCK = 128
H_A, KV_A, DQK_A = 8, 2, 64
G_A = H_A // KV_A
DV_A = 2 * DQK_A
H_B, KV_B, D_B = 8, 2, 128
G_B = H_B // KV_B
H_C, NOPE_C, ROPE_C, V_C = 16, 128, 64, 128
Q_LORA, KV_LORA = 512, 256
ROPE_THETA = 10000.0
N_GROUPS, EXP_PER_GROUP = 4, 8
N_EXPERTS = N_GROUPS * EXP_PER_GROUP
D_EXPERT = 512
DEPTH = 2
ALPHA = (2 * DEPTH) ** 0.25
DIFF_SCALE = DQK_A ** -0.5
FOX_SCALE = D_B ** -0.5
MLA_SCALE = (NOPE_C + ROPE_C) ** -0.5
LN_EPS = 1e-5
RMS_EPS = 1e-6

OFF_QD = 0
OFF_KD = H_A * 2 * DQK_A
OFF_VD = OFF_KD + KV_A * 2 * DQK_A
OFF_QF = OFF_VD + KV_A * DV_A
OFF_KF = OFF_QF + H_B * D_B
OFF_VF = OFF_KF + KV_B * D_B
OFF_FL = OFF_VF + KV_B * D_B
LANE = 128
VMEM_LIMIT = 56 * 2 ** 20

NT_DIMS = (((1,), (1,)), ((), ()))
NEG_INF = float("-inf")


def _cparams(sem):
    return pltpu.CompilerParams(dimension_semantics=sem, vmem_limit_bytes=VMEM_LIMIT)


def _dot(a, b):
    return jnp.dot(a, b, preferred_element_type=F32)


def _dot_nt(a, b):
    return lax.dot_general(a, b, NT_DIMS, preferred_element_type=F32)


def _split3(v):
    hi = v.astype(BF16)
    r1 = v - hi.astype(F32)
    mid = r1.astype(BF16)
    lo = (r1 - mid.astype(F32)).astype(BF16)
    return hi, mid, lo


def _mm_body(x_ref, w_ref, s_ref, o_ref, ob_ref, *, nt):
    x = x_ref[...].astype(BF16)
    w = w_ref[...].astype(BF16)
    acc = _dot_nt(x, w) if nt else _dot(x, w)
    o_ref[...] = acc
    ob_ref[...] = (acc * s_ref[...]).astype(BF16)


def matmul_scaled(x, w, col_scale, *, nt=False, n_out=None, tm=512, tn=512, name="matmul"):
    m, k = x.shape
    n = n_out if n_out is not None else (w.shape[0] if nt else w.shape[1])
    tm, tn = min(tm, m), min(tn, n)
    assert m % tm == 0 and n % tn == 0
    w_spec = pl.BlockSpec((tn, k), lambda i, j: (j, 0)) if nt else pl.BlockSpec((k, tn), lambda i, j: (0, j))
    out_spec = pl.BlockSpec((tm, tn), lambda i, j: (i, j))
    return pl.pallas_call(
        functools.partial(_mm_body, nt=nt),
        grid=(m // tm, n // tn),
        in_specs=[pl.BlockSpec((tm, k), lambda i, j: (i, 0)), w_spec, pl.BlockSpec((1, tn), lambda i, j: (0, j))],
        out_specs=[out_spec, out_spec],
        out_shape=[jax.ShapeDtypeStruct((m, n), F32), jax.ShapeDtypeStruct((m, n), BF16)],
        compiler_params=_cparams(("parallel", "parallel")),
        name=name,
    )(x, w, col_scale)


def _layer_norm(v, g, b):
    mu = jnp.mean(v, axis=-1, keepdims=True)
    d = v - mu
    var = jnp.mean(d * d, axis=-1, keepdims=True)
    return d * lax.rsqrt(var + LN_EPS) * g + b


def _mm_ln_body(x_ref, w_ref, r_ref, g_ref, b_ref, o_ref, acc_ref):
    k = pl.program_id(1)

    @pl.when(k == 0)
    def _():
        acc_ref[...] = jnp.zeros_like(acc_ref)

    acc_ref[...] += _dot(x_ref[...].astype(BF16), w_ref[...].astype(BF16))

    @pl.when(k == pl.num_programs(1) - 1)
    def _():
        o_ref[...] = _layer_norm(ALPHA * r_ref[...] + acc_ref[...], g_ref[...], b_ref[...])


def matmul_ln(x, w, resid, g, b, *, tm=512, tk=512, name="proj_ln"):
    m, kdim = x.shape
    n = w.shape[1]
    tm = min(tm, m)
    return pl.pallas_call(
        _mm_ln_body,
        grid=(m // tm, kdim // tk),
        in_specs=[pl.BlockSpec((tm, tk), lambda i, k: (i, k)), pl.BlockSpec((tk, n), lambda i, k: (k, 0)),
                  pl.BlockSpec((tm, n), lambda i, k: (i, 0)), pl.BlockSpec((1, n), lambda i, k: (0, 0)),
                  pl.BlockSpec((1, n), lambda i, k: (0, 0))],
        out_specs=pl.BlockSpec((tm, n), lambda i, k: (i, 0)),
        out_shape=jax.ShapeDtypeStruct((m, n), F32),
        scratch_shapes=[pltpu.VMEM((tm, n), F32)],
        compiler_params=_cparams(("parallel", "arbitrary")),
        name=name,
    )(x, w, resid, g, b)


def _logsig(z):
    return jnp.minimum(z, 0.0) - jnp.log1p(jnp.exp(-jnp.abs(z)))


def _forget_body(x_ref, wf_ref, wft_ref, b_ref, bt_ref, lf_ref, cum_ref, cumt_ref, carry_ref, carryt_ref, *, tm, seg):
    i = pl.program_id(0)

    @pl.when((i * tm) % seg == 0)
    def _():
        carry_ref[...] = jnp.zeros_like(carry_ref)
        carryt_ref[...] = jnp.zeros_like(carryt_ref)

    xb = x_ref[...].astype(BF16)
    lf = _logsig(_dot(xb, wf_ref[...].astype(BF16)) + b_ref[...])
    lft = _logsig(_dot_nt(wft_ref[...].astype(BF16), xb) + bt_ref[...])
    lf_ref[...] = lf
    r = lax.broadcasted_iota(jnp.int32, (tm, tm), 0)
    c = lax.broadcasted_iota(jnp.int32, (tm, tm), 1)
    same = (r // seg) == (c // seg) if seg < tm else (r >= 0)
    lower = jnp.where((c <= r) & same, 1.0, 0.0).astype(BF16)
    upper = jnp.where((r <= c) & same, 1.0, 0.0).astype(BF16)
    cum = carry_ref[0:1, :]
    for part in _split3(lf):
        cum = cum + _dot(lower, part)
    cumt = carryt_ref[:, 0:1]
    for part in _split3(lft):
        cumt = cumt + _dot(part, upper)
    cum_ref[...] = cum
    cumt_ref[...] = cumt
    carry_ref[...] = jnp.broadcast_to(cum[tm - 1:tm, :], carry_ref.shape)
    carryt_ref[...] = jnp.broadcast_to(cumt[:, tm - 1:tm], carryt_ref.shape)


def forget_gates(x, wf_pad, wft, b_row, b_col, *, seg, tm=256):
    m, k = x.shape
    tm = min(tm, m)
    assert m % tm == 0 and (seg % tm == 0 or tm % seg == 0)
    fixed = lambda i: (0, 0)
    return pl.pallas_call(
        functools.partial(_forget_body, tm=tm, seg=seg),
        grid=(m // tm,),
        in_specs=[pl.BlockSpec((tm, k), lambda i: (i, 0)), pl.BlockSpec((k, LANE), fixed), pl.BlockSpec((H_B, k), fixed),
                  pl.BlockSpec((1, LANE), fixed), pl.BlockSpec((H_B, 1), fixed)],
        out_specs=[pl.BlockSpec((tm, LANE), lambda i: (i, 0)), pl.BlockSpec((tm, LANE), lambda i: (i, 0)),
                   pl.BlockSpec((H_B, tm), lambda i: (0, i))],
        out_shape=[jax.ShapeDtypeStruct((m, LANE), F32), jax.ShapeDtypeStruct((m, LANE), F32),
                   jax.ShapeDtypeStruct((H_B, m), F32)],
        scratch_shapes=[pltpu.VMEM((8, LANE), F32), pltpu.VMEM((H_B, LANE), F32)],
        compiler_params=_cparams(("arbitrary",)),
        name="forget_gates",
    )(x, wf_pad, wft, b_row, b_col)


def _online_update(s, v_bf, m_ref, l_ref, acc_ref, idx=None):
    sel = (lambda r: r) if idx is None else (lambda r: r.at[idx])
    m_prev = sel(m_ref)[...]
    m_new = jnp.maximum(m_prev, jnp.max(s, axis=-1, keepdims=True))
    alpha = jnp.exp(m_prev - m_new)
    p = jnp.exp(s - m_new)
    sel(l_ref)[...] = alpha * sel(l_ref)[...] + jnp.sum(p, axis=-1, keepdims=True)
    sel(acc_ref)[...] = alpha * sel(acc_ref)[...] + _dot(p.astype(BF16), v_bf)
    sel(m_ref)[...] = m_new


def _softmax_step(s, v_bf, m_prev, acc_prev, l_prev=None):
    m_new = jnp.maximum(m_prev, jnp.max(s, axis=-1, keepdims=True))
    alpha = jnp.exp(m_prev - m_new)
    p = jnp.exp(s - m_new)
    acc_new = alpha * acc_prev + _dot(p.astype(BF16), v_bf)
    if l_prev is None:
        return m_new, acc_new
    return m_new, acc_new, alpha * l_prev + jnp.sum(p, axis=-1, keepdims=True)


def _with_ones(v_bf):
    return jnp.concatenate([v_bf, jnp.ones_like(v_bf)], axis=1)


def _diff_lambda(lq1, lk1, lq2, lk2, lam_init):
    s1 = jnp.sum(lq1 * lk1, axis=-1, keepdims=True)
    s2 = jnp.sum(lq2 * lk2, axis=-1, keepdims=True)
    return jnp.exp(s1) - jnp.exp(s2) + lam_init


def _subln(o, g, lam_init):
    ms = jnp.mean(o * o, axis=-1, keepdims=True)
    return o * lax.rsqrt(ms + LN_EPS) * g * (1.0 - lam_init)


ROW_SPLIT = 4
FOX_SPLIT = 4
MLA_SPLIT = 4


def _causal_tile(t):
    return lax.broadcasted_iota(jnp.int32, (t, t), 0) >= lax.broadcasted_iota(jnp.int32, (t, t), 1)


def _pair_tables(n_tiles):
    qi = np.array([i for i in range(n_tiles) for _ in range(i + 1)], np.int32)
    kj = np.array([j for i in range(n_tiles) for j in range(i + 1)], np.int32)
    return jnp.asarray(qi), jnp.asarray(kj)


def _diff_prompt_body(qi_ref, kj_ref, sl_ref, q_ref, k_ref, v_ref, lq1, lk1, lq2, lk2, g_ref, o_ref,
                      m_ref, acc_ref, *, t, lam_init):
    h = pl.program_id(0)
    n = pl.program_id(1)
    i = qi_ref[n]
    j = kj_ref[n]

    @pl.when(j == 0)
    def _():
        m_ref[...] = jnp.full_like(m_ref, NEG_INF)
        acc_ref[...] = jnp.zeros_like(acc_ref)

    q = q_ref[...]
    lane = lax.broadcasted_iota(jnp.int32, q.shape, 1)
    kb = k_ref[...]
    v1 = _with_ones(v_ref[...])
    kcol = lax.broadcasted_iota(jnp.int32, (1, t), 1)
    bias = sl_ref[h] * ((j - i) * t + kcol).astype(F32)

    def update(masked):
        sub = t // ROW_SPLIT
        res = []
        for c in range(2):
            qc = jnp.where((lane >= c * DQK_A) & (lane < (c + 1) * DQK_A), q, jnp.zeros_like(q))
            for r in range(ROW_SPLIT):
                rows = slice(r * sub, (r + 1) * sub)
                sc = _dot_nt(qc[rows], kb) + bias
                if masked:
                    sc = jnp.where(_causal_tile(t)[rows], sc, NEG_INF)
                res.append((c, rows, _softmax_step(sc, v1, m_ref[c, rows], acc_ref[c, rows])))
        for c, rows, (m_new, acc_new) in res:
            m_ref[c, rows] = m_new
            acc_ref[c, rows] = acc_new

    @pl.when(j < i)
    def _():
        update(False)

    @pl.when(j == i)
    def _():
        update(True)
        lam = _diff_lambda(lq1[...], lk1[...], lq2[...], lk2[...], lam_init)
        a0, a1 = acc_ref[0], acc_ref[1]
        o = a0[:, :DV_A] / a0[:, DV_A:DV_A + 1] - lam * (a1[:, :DV_A] / a1[:, DV_A:DV_A + 1])
        o_ref[...] = _subln(o, g_ref[...], lam_init)


def diff_prompt(h, slopes, lq1, lk1, lq2, lk2, g_subln, lam_init, *, t=1024):
    s = h.shape[0]
    t = min(t, s)
    qi, kj = _pair_tables(s // t)
    cb = lambda off: off // LANE
    small = lambda hh, n, qi, kj, sl: (0, 0)
    return pl.pallas_call(
        functools.partial(_diff_prompt_body, t=t, lam_init=lam_init),
        grid_spec=pltpu.PrefetchScalarGridSpec(
            num_scalar_prefetch=3,
            grid=(H_A, qi.shape[0]),
            in_specs=[pl.BlockSpec((t, LANE), lambda hh, n, qi, kj, sl: (qi[n], cb(OFF_QD) + hh)),
                      pl.BlockSpec((t, LANE), lambda hh, n, qi, kj, sl: (kj[n], cb(OFF_KD) + hh // G_A)),
                      pl.BlockSpec((t, LANE), lambda hh, n, qi, kj, sl: (kj[n], cb(OFF_VD) + hh // G_A)),
                      pl.BlockSpec((1, DQK_A), small), pl.BlockSpec((1, DQK_A), small),
                      pl.BlockSpec((1, DQK_A), small), pl.BlockSpec((1, DQK_A), small),
                      pl.BlockSpec((1, DV_A), small)],
            out_specs=pl.BlockSpec((t, LANE), lambda hh, n, qi, kj, sl: (qi[n], hh)),
            scratch_shapes=[pltpu.VMEM((2, t, 1), F32), pltpu.VMEM((2, t, 2 * DV_A), F32)]),
        out_shape=jax.ShapeDtypeStruct((s, H_A * DV_A), F32),
        compiler_params=_cparams(("parallel", "arbitrary")),
        name="diff_prompt",
    )(qi, kj, slopes, h, h, h, lq1, lk1, lq2, lk2, g_subln)


def _fox_prompt_body(qi_ref, kj_ref, q_ref, k_ref, v_ref, cq_ref, ck_ref, o_ref, m_ref, acc_ref, *, t):
    h = pl.program_id(0)
    n = pl.program_id(1)
    i = qi_ref[n]
    j = kj_ref[n]

    @pl.when(j == 0)
    def _():
        m_ref[...] = jnp.full_like(m_ref, NEG_INF)
        acc_ref[...] = jnp.zeros_like(acc_ref)

    qb = q_ref[...]
    kb = k_ref[...]
    v1 = _with_ones(v_ref[...])
    lane = lax.broadcasted_iota(jnp.int32, (1, LANE), 1)
    base = jnp.sum(jnp.where(lane == h, cq_ref[0:1, :], 0.0), axis=-1, keepdims=True)
    bias = base - ck_ref[...]

    def update(masked):
        sub = t // FOX_SPLIT
        res = []
        for r in range(FOX_SPLIT):
            rows = slice(r * sub, (r + 1) * sub)
            sc = _dot_nt(qb[rows], kb) + bias
            if masked:
                sc = jnp.where(_causal_tile(t)[rows], sc, NEG_INF)
            res.append((rows, _softmax_step(sc, v1, m_ref[rows], acc_ref[rows])))
        for rows, (m_new, acc_new) in res:
            m_ref[rows] = m_new
            acc_ref[rows] = acc_new

    @pl.when(j < i)
    def _():
        update(False)

    @pl.when(j == i)
    def _():
        update(True)
        acc = acc_ref[...]
        o_ref[...] = acc[:, :D_B] / acc[:, D_B:D_B + 1]


def fox_prompt(h, cum, cumt3, *, t=1024):
    s = h.shape[0]
    t = min(t, s)
    qi, kj = _pair_tables(s // t)
    cb = lambda off: off // LANE
    return pl.pallas_call(
        functools.partial(_fox_prompt_body, t=t),
        grid_spec=pltpu.PrefetchScalarGridSpec(
            num_scalar_prefetch=2,
            grid=(H_B, qi.shape[0]),
            in_specs=[pl.BlockSpec((t, LANE), lambda hh, n, qi, kj: (qi[n], cb(OFF_QF) + hh)),
                      pl.BlockSpec((t, LANE), lambda hh, n, qi, kj: (kj[n], cb(OFF_KF) + hh // G_B)),
                      pl.BlockSpec((t, LANE), lambda hh, n, qi, kj: (kj[n], cb(OFF_VF) + hh // G_B)),
                      pl.BlockSpec((t, LANE), lambda hh, n, qi, kj: (qi[n], 0)),
                      pl.BlockSpec((None, 1, t), lambda hh, n, qi, kj: (hh, 0, kj[n]))],
            out_specs=pl.BlockSpec((t, LANE), lambda hh, n, qi, kj: (qi[n], hh)),
            scratch_shapes=[pltpu.VMEM((t, 1), F32), pltpu.VMEM((t, 2 * D_B), F32)]),
        out_shape=jax.ShapeDtypeStruct((s, H_B * D_B), F32),
        compiler_params=_cparams(("parallel", "arbitrary")),
        name="fox_prompt",
    )(qi, kj, h, h, h, cum, cumt3)


CHUNK_PAGES = 16
CHUNK = CHUNK_PAGES * PAGE_SIZE
T_S = 8


def _chunk_schedule(pt_ref, b, k, n_chunks, n_seq):
    c = n_chunks - 1 - k
    last = k == n_chunks - 1
    nb = jnp.where(last, b + 1, b)
    nc = jnp.where(last, n_chunks - 1, c - 1)
    return c, nb, nc, jnp.logical_or(jnp.logical_not(last), b + 1 < n_seq)


def _even_copies(pt_ref, caches, bufs, sem, b, c, slot):
    out = []
    for p in range(CHUNK_PAGES):
        page = pt_ref[b, c * CHUNK_PAGES + p]
        for a, (src, dst) in enumerate(zip(caches, bufs)):
            out.append(pltpu.make_async_copy(src.at[page], dst.at[slot, p], sem.at[slot, a]))
    return out


def _even_sample_body(pt_ref, hs_ref, cs_ref, cst_ref, lq1, lk1, lq2, lk2, g_ref,
                      dk_hbm, dv_hbm, fk_hbm, fv_hbm, lf_hbm, od_ref, of_ref,
                      dkbuf, dvbuf, fkbuf, fvbuf, lfbuf, sem, ktb, vdb, kfb, vfb,
                      md, ld, accd, mf, lff, accf, run_ref, *, n_chunks, n_seq, past_len, lam_init):
    b = pl.program_id(0)
    caches = (dk_hbm, dv_hbm, fk_hbm, fv_hbm, lf_hbm)
    bufs = (dkbuf, dvbuf, fkbuf, fvbuf, lfbuf)
    n_d, n_f = 2 * KV_A * G_A * T_S, KV_B * G_B * T_S

    @pl.when(b == 0)
    def _():
        for cp in _even_copies(pt_ref, caches, bufs, sem, 0, n_chunks - 1, 0):
            cp.start()

    hs = hs_ref[...]
    lane = lax.broadcasted_iota(jnp.int32, (T_S, LANE), 1)
    zero = jnp.zeros((T_S, LANE), F32)
    rows = []
    for kv in range(KV_A):
        for c in range(2):
            for g in range(G_A):
                qh = hs[:, OFF_QD + (kv * G_A + g) * LANE:OFF_QD + (kv * G_A + g + 1) * LANE] * DIFF_SCALE
                qh = jnp.where((lane >= c * DQK_A) & (lane < (c + 1) * DQK_A), qh, 0.0)
                rows.append(jnp.concatenate([qh, zero] if kv == 0 else [zero, qh], axis=1))
    qd = jnp.concatenate(rows, axis=0).astype(BF16)
    rows = []
    for hh in range(H_B):
        qh = hs[:, OFF_QF + hh * LANE:OFF_QF + (hh + 1) * LANE] * FOX_SCALE
        rows.append(jnp.concatenate([qh, zero] if hh // G_B == 0 else [zero, qh], axis=1))
    qf = jnp.concatenate(rows, axis=0).astype(BF16)

    rd = lax.broadcasted_iota(jnp.int32, (n_d, 1), 0)
    head_d = (rd // (2 * G_A * T_S)) * G_A + (rd // T_S) % G_A
    slope = jnp.exp2(-(head_d + 1).astype(F32))
    t_d = rd % T_S
    qpos_d = (past_len + t_d).astype(F32)
    rf = lax.broadcasted_iota(jnp.int32, (n_f, 1), 0)
    t_f = rf % T_S
    cs = cs_ref[...]
    cq = jnp.concatenate([cs[:, hh:hh + 1] for hh in range(H_B)], axis=0)

    md[...] = jnp.full_like(md, NEG_INF)
    ld[...] = jnp.zeros_like(ld)
    accd[...] = jnp.zeros_like(accd)
    mf[...] = jnp.full_like(mf, NEG_INF)
    lff[...] = jnp.zeros_like(lff)
    accf[...] = jnp.zeros_like(accf)
    run_ref[...] = jnp.zeros_like(run_ref)

    pad = jnp.zeros((LANE - T_S, 2 * LANE), F32)
    key = lax.broadcasted_iota(jnp.int32, (1, LANE), 1)
    kd_new = jnp.concatenate([hs[:, OFF_KD:OFF_KD + 2 * LANE], pad], axis=0).astype(BF16)
    vd_new = jnp.concatenate([hs[:, OFF_VD:OFF_VD + 2 * LANE], pad], axis=0).astype(BF16)
    kf_new = jnp.concatenate([hs[:, OFF_KF:OFF_KF + 2 * LANE], pad], axis=0).astype(BF16)
    vf_new = jnp.concatenate([hs[:, OFF_VF:OFF_VF + 2 * LANE], pad], axis=0).astype(BF16)
    s = _dot_nt(qd, kd_new) - slope * (t_d - key).astype(F32)
    _online_update(jnp.where(key <= t_d, s, NEG_INF), vd_new, md, ld, accd)
    cst = cst_ref[...]
    cst_rows = jnp.broadcast_to(cst[:, None, :], (H_B, T_S, LANE)).reshape(n_f, LANE)
    s = _dot_nt(qf, kf_new) + (cq - cst_rows)
    _online_update(jnp.where(key <= t_f, s, NEG_INF), vf_new, mf, lff, accf)

    jj = lax.broadcasted_iota(jnp.int32, (PAGE_SIZE, PAGE_SIZE), 0)
    ss = lax.broadcasted_iota(jnp.int32, (PAGE_SIZE, PAGE_SIZE), 1)
    later = jnp.where(jj > ss, 1.0, 0.0).astype(BF16)
    kidx = lax.broadcasted_iota(jnp.int32, (1, CHUNK), 1)

    def step(k, carry):
        c, nb, nc, has_next = _chunk_schedule(pt_ref, b, k, n_chunks, n_seq)
        slot = k % 2

        @pl.when(has_next)
        def _():
            for cp in _even_copies(pt_ref, caches, bufs, sem, nb, nc, 1 - slot):
                cp.start()

        for cp in _even_copies(pt_ref, caches, bufs, sem, b, c, slot):
            cp.wait()

        for p in range(CHUNK_PAGES):
            lo, hi = p * PAGE_SIZE, (p + 1) * PAGE_SIZE
            ktb[:, lo:hi] = dkbuf[slot, p].astype(BF16)
            for kv in range(2):
                sl = pl.ds(kv, PAGE_SIZE, stride=2)
                vdb[lo:hi, kv * LANE:(kv + 1) * LANE] = dvbuf[slot, p, sl, :].astype(BF16)
                kfb[lo:hi, kv * LANE:(kv + 1) * LANE] = fkbuf[slot, p, sl, :].astype(BF16)
                vfb[lo:hi, kv * LANE:(kv + 1) * LANE] = fvbuf[slot, p, sl, :].astype(BF16)

        kpos = (c * CHUNK + kidx).astype(F32)
        s = _dot(qd, ktb[...]) - slope * (qpos_d - kpos)
        _online_update(s, vdb[...], md, ld, accd)

        x = lfbuf[slot].reshape(CHUNK_PAGES * H_B, PAGE_SIZE)
        within = jnp.zeros_like(x)
        for part in _split3(x):
            within = within + _dot(part, later)
        tot = jnp.sum(x, axis=-1, keepdims=True)
        run = run_ref[:, 0:1]
        pieces = [None] * CHUNK_PAGES
        for p in reversed(range(CHUNK_PAGES)):
            after = within[p * H_B:(p + 1) * H_B] + run
            pieces[p] = jnp.broadcast_to(after[:, None, :], (H_B, T_S, PAGE_SIZE)).reshape(n_f, PAGE_SIZE)
            run = run + tot[p * H_B:(p + 1) * H_B]
        run_ref[...] = jnp.broadcast_to(run, run_ref.shape)
        s = _dot_nt(qf, kfb[...]) + (cq + jnp.concatenate(pieces, axis=1))
        _online_update(s, vfb[...], mf, lff, accf)
        return carry

    lax.fori_loop(0, n_chunks, step, 0)

    lam = _diff_lambda(lq1[...], lk1[...], lq2[...], lk2[...], lam_init)
    od = accd[...] / ld[...]
    g = g_ref[...]
    for kv in range(KV_A):
        for gg in range(G_A):
            r0 = ((kv * 2 + 0) * G_A + gg) * T_S
            r1 = ((kv * 2 + 1) * G_A + gg) * T_S
            o = od[r0:r0 + T_S, kv * LANE:(kv + 1) * LANE] - lam * od[r1:r1 + T_S, kv * LANE:(kv + 1) * LANE]
            od_ref[:, (kv * G_A + gg) * LANE:(kv * G_A + gg + 1) * LANE] = _subln(o, g, lam_init)
    of = accf[...] / lff[...]
    for hh in range(H_B):
        kv = hh // G_B
        of_ref[:, hh * LANE:(hh + 1) * LANE] = of[hh * T_S:(hh + 1) * T_S, kv * LANE:(kv + 1) * LANE]


def even_sample(page_table, hs, cs, cst, lq1, lk1, lq2, lk2, g_subln, dk_t, dv2, fk2, fv2, lf_t, lam_init):
    n_seq, n_pages = page_table.shape
    assert n_pages % (2 * CHUNK_PAGES) == 0 and hs.shape[0] == n_seq * T_S
    n_chunks = n_pages // CHUNK_PAGES
    width = hs.shape[1]
    small = lambda b, pt: (0, 0)
    anyspec = pl.BlockSpec(memory_space=pl.ANY)
    page_buf = lambda: pltpu.VMEM((2, CHUNK_PAGES, 2 * LANE, PAGE_SIZE), F32)
    n_d, n_f = 2 * KV_A * G_A * T_S, KV_B * G_B * T_S
    return pl.pallas_call(
        functools.partial(_even_sample_body, n_chunks=n_chunks, n_seq=n_seq, past_len=n_pages * PAGE_SIZE, lam_init=lam_init),
        grid_spec=pltpu.PrefetchScalarGridSpec(
            num_scalar_prefetch=1,
            grid=(n_seq,),
            in_specs=[pl.BlockSpec((T_S, width), lambda b, pt: (b, 0)),
                      pl.BlockSpec((T_S, LANE), lambda b, pt: (b, 0)),
                      pl.BlockSpec((None, H_B, LANE), lambda b, pt: (b, 0, 0)),
                      pl.BlockSpec((1, DQK_A), small), pl.BlockSpec((1, DQK_A), small),
                      pl.BlockSpec((1, DQK_A), small), pl.BlockSpec((1, DQK_A), small),
                      pl.BlockSpec((1, DV_A), small),
                      anyspec, anyspec, anyspec, anyspec, anyspec],
            out_specs=[pl.BlockSpec((T_S, H_A * DV_A), lambda b, pt: (b, 0)),
                       pl.BlockSpec((T_S, H_B * D_B), lambda b, pt: (b, 0))],
            scratch_shapes=[page_buf(), page_buf(), page_buf(), page_buf(),
                            pltpu.VMEM((2, CHUNK_PAGES, H_B, PAGE_SIZE), F32),
                            pltpu.SemaphoreType.DMA((2, 5)),
                            pltpu.VMEM((2 * LANE, CHUNK), BF16), pltpu.VMEM((CHUNK, 2 * LANE), BF16),
                            pltpu.VMEM((CHUNK, 2 * LANE), BF16), pltpu.VMEM((CHUNK, 2 * LANE), BF16),
                            pltpu.VMEM((n_d, 1), F32), pltpu.VMEM((n_d, 1), F32), pltpu.VMEM((n_d, 2 * LANE), F32),
                            pltpu.VMEM((n_f, 1), F32), pltpu.VMEM((n_f, 1), F32), pltpu.VMEM((n_f, 2 * LANE), F32),
                            pltpu.VMEM((H_B, LANE), F32)]),
        out_shape=[jax.ShapeDtypeStruct((n_seq * T_S, H_A * DV_A), F32), jax.ShapeDtypeStruct((n_seq * T_S, H_B * D_B), F32)],
        compiler_params=_cparams(("arbitrary",)),
        name="even_sample",
    )(page_table, hs, cs, cst, lq1, lk1, lq2, lk2, g_subln, dk_t, dv2, fk2, fv2, lf_t)


MLA_CHUNK_PAGES = 32
MLA_CHUNK = MLA_CHUNK_PAGES * PAGE_SIZE


def _mla_copies(pt_ref, caches, bufs, sem, b, c, slot):
    out = []
    for p in range(MLA_CHUNK_PAGES):
        page = pt_ref[b, c * MLA_CHUNK_PAGES + p]
        for a, (src, dst) in enumerate(zip(caches, bufs)):
            out.append(pltpu.make_async_copy(src.at[page], dst.at[slot, p], sem.at[slot, a]))
    return out


def _mla_sample_body(pt_ref, q_ref, ckvn_ref, kpen_ref, ckv_hbm, kpe_hbm, o_ref,
                     ckvbuf, kpebuf, sem, ckvb, kpeb, m_ref, l_ref, acc_ref, *, n_chunks, n_seq):
    b = pl.program_id(0)
    caches = (ckv_hbm, kpe_hbm)
    bufs = (ckvbuf, kpebuf)
    n_r = H_C * T_S

    @pl.when(b == 0)
    def _():
        for cp in _mla_copies(pt_ref, caches, bufs, sem, 0, n_chunks - 1, 0):
            cp.start()

    q = q_ref[...].reshape(n_r, KV_LORA + ROPE_C) * MLA_SCALE
    ql = q[:, :KV_LORA].astype(BF16)
    qp = q[:, KV_LORA:].astype(BF16)
    t_r = lax.broadcasted_iota(jnp.int32, (n_r, 1), 0) % T_S
    key = lax.broadcasted_iota(jnp.int32, (1, LANE), 1)

    m_ref[...] = jnp.full_like(m_ref, NEG_INF)
    l_ref[...] = jnp.zeros_like(l_ref)
    acc_ref[...] = jnp.zeros_like(acc_ref)

    ckv_new = jnp.concatenate([ckvn_ref[...], jnp.zeros((LANE - T_S, KV_LORA), F32)], axis=0).astype(BF16)
    kpe_new = jnp.concatenate([kpen_ref[...], jnp.zeros((LANE - T_S, ROPE_C), F32)], axis=0).astype(BF16)
    s = _dot_nt(ql, ckv_new) + _dot_nt(qp, kpe_new)
    _online_update(jnp.where(key <= t_r, s, NEG_INF), ckv_new, m_ref, l_ref, acc_ref)

    def step(k, carry):
        c, nb, nc, has_next = _chunk_schedule(pt_ref, b, k, n_chunks, n_seq)
        slot = k % 2

        @pl.when(has_next)
        def _():
            for cp in _mla_copies(pt_ref, caches, bufs, sem, nb, nc, 1 - slot):
                cp.start()

        for cp in _mla_copies(pt_ref, caches, bufs, sem, b, c, slot):
            cp.wait()

        for p in range(MLA_CHUNK_PAGES):
            lo, hi = p * PAGE_SIZE, (p + 1) * PAGE_SIZE
            ckvb[lo:hi, :] = ckvbuf[slot, p].astype(BF16)
            kpeb[:, lo:hi] = kpebuf[slot, p].astype(BF16)

        half = MLA_CHUNK // 2
        parts = []
        for hlf in range(2):
            ks = slice(hlf * half, (hlf + 1) * half)
            kv = ckvb[ks, :]
            s = _dot_nt(ql, kv) + _dot(qp, kpeb[:, ks])
            m_h = jnp.max(s, axis=-1, keepdims=True)
            p = jnp.exp(s - m_h)
            parts.append((m_h, jnp.sum(p, axis=-1, keepdims=True), _dot(p.astype(BF16), kv)))
        m_prev = m_ref[...]
        m_new = jnp.maximum(m_prev, jnp.maximum(parts[0][0], parts[1][0]))
        w_prev = jnp.exp(m_prev - m_new)
        l_new = w_prev * l_ref[...]
        acc_new = w_prev * acc_ref[...]
        for m_h, l_h, acc_h in parts:
            w_h = jnp.exp(m_h - m_new)
            l_new = l_new + w_h * l_h
            acc_new = acc_new + w_h * acc_h
        m_ref[...] = m_new
        l_ref[...] = l_new
        acc_ref[...] = acc_new
        return carry

    lax.fori_loop(0, n_chunks, step, 0)
    o_ref[...] = (acc_ref[...] / l_ref[...]).reshape(H_C, T_S, KV_LORA)


def mla_sample(page_table, q3, ckv_new, kpe_new, ckv_pages, kpe_t):
    n_seq, n_pages = page_table.shape
    assert n_pages % (2 * MLA_CHUNK_PAGES) == 0
    n_chunks = n_pages // MLA_CHUNK_PAGES
    anyspec = pl.BlockSpec(memory_space=pl.ANY)
    n_r = H_C * T_S
    return pl.pallas_call(
        functools.partial(_mla_sample_body, n_chunks=n_chunks, n_seq=n_seq),
        grid_spec=pltpu.PrefetchScalarGridSpec(
            num_scalar_prefetch=1,
            grid=(n_seq,),
            in_specs=[pl.BlockSpec((H_C, T_S, KV_LORA + ROPE_C), lambda b, pt: (0, b, 0)),
                      pl.BlockSpec((T_S, KV_LORA), lambda b, pt: (b, 0)),
                      pl.BlockSpec((T_S, ROPE_C), lambda b, pt: (b, 0)),
                      anyspec, anyspec],
            out_specs=pl.BlockSpec((H_C, T_S, KV_LORA), lambda b, pt: (0, b, 0)),
            scratch_shapes=[pltpu.VMEM((2, MLA_CHUNK_PAGES, PAGE_SIZE, KV_LORA), F32),
                            pltpu.VMEM((2, MLA_CHUNK_PAGES, ROPE_C, PAGE_SIZE), F32),
                            pltpu.SemaphoreType.DMA((2, 2)),
                            pltpu.VMEM((MLA_CHUNK, KV_LORA), BF16), pltpu.VMEM((ROPE_C, MLA_CHUNK), BF16),
                            pltpu.VMEM((n_r, 1), F32), pltpu.VMEM((n_r, 1), F32), pltpu.VMEM((n_r, KV_LORA), F32)]),
        out_shape=jax.ShapeDtypeStruct((H_C, n_seq * T_S, KV_LORA), F32),
        compiler_params=_cparams(("arbitrary",)),
        name="mla_sample",
    )(page_table, q3, ckv_new, kpe_new, ckv_pages, kpe_t)


def _rope(x, cos2, sin2):
    half = x.shape[-1] // 2
    swapped = jnp.concatenate([x[:, half:], x[:, :half]], axis=1)
    return x * cos2 + swapped * sin2


def _rms(v, g, eps):
    return v * lax.rsqrt(jnp.mean(v * v, axis=-1, keepdims=True) + eps) * g


def _mla_cq_body(x_ref, w_ref, g_ref, o_ref):
    cq = _dot(x_ref[...].astype(BF16), w_ref[...].astype(BF16))
    o_ref[...] = _rms(cq, g_ref[...], RMS_EPS).astype(o_ref.dtype)


def mla_cq(x, w_dq, g_qn, *, tm=512):
    m, k = x.shape
    tm = min(tm, m)
    return pl.pallas_call(
        _mla_cq_body,
        grid=(m // tm,),
        in_specs=[pl.BlockSpec((tm, k), lambda i: (i, 0)), pl.BlockSpec((k, Q_LORA), lambda i: (0, 0)),
                  pl.BlockSpec((1, Q_LORA), lambda i: (0, 0))],
        out_specs=pl.BlockSpec((tm, Q_LORA), lambda i: (i, 0)),
        out_shape=jax.ShapeDtypeStruct((m, Q_LORA), BF16),
        compiler_params=_cparams(("parallel",)),
        name="mla_cq",
    )(x, w_dq, g_qn)


def _mla_q_body(cq_ref, wuq_ref, wuk_ref, cos_ref, sin_ref, o_ref, *, scale, absorb):
    q = _dot_nt(cq_ref[...].astype(BF16), wuq_ref[...].astype(BF16))
    q_pe = (_rope(q[:, NOPE_C:], cos_ref[...], sin_ref[...]) * scale).astype(o_ref.dtype)
    if absorb:
        q_lat = _dot_nt(q[:, :NOPE_C].astype(BF16), wuk_ref[...].astype(BF16))
        o_ref[:, :KV_LORA] = (q_lat * scale).astype(o_ref.dtype)
        o_ref[:, KV_LORA:] = q_pe
    else:
        o_ref[:, :NOPE_C] = (q[:, :NOPE_C] * scale).astype(o_ref.dtype)
        o_ref[:, NOPE_C:] = q_pe


def mla_q(cq, wuq_t, w_ukv2, cos2, sin2, *, scale=1.0, out_dtype=F32, absorb=True, tm=2048):
    m = cq.shape[0]
    tm = min(tm, m)
    width = (KV_LORA if absorb else NOPE_C) + ROPE_C
    return pl.pallas_call(
        functools.partial(_mla_q_body, scale=scale, absorb=absorb),
        grid=(H_C, m // tm),
        in_specs=[pl.BlockSpec((tm, Q_LORA), lambda h, i: (i, 0)),
                  pl.BlockSpec((None, NOPE_C + ROPE_C, Q_LORA), lambda h, i: (h, 0, 0)),
                  pl.BlockSpec((KV_LORA, NOPE_C), lambda h, i: (0, 2 * h)),
                  pl.BlockSpec((tm, ROPE_C), lambda h, i: (i, 0)), pl.BlockSpec((tm, ROPE_C), lambda h, i: (i, 0))],
        out_specs=pl.BlockSpec((None, tm, width), lambda h, i: (h, i, 0)),
        out_shape=jax.ShapeDtypeStruct((H_C, m, width), out_dtype),
        compiler_params=_cparams(("parallel", "parallel")),
        name="mla_q",
    )(cq, wuq_t, w_ukv2, cos2, sin2)


def _mla_kv_body(x_ref, w_ref, g_ref, cos_ref, sin_ref, ckv_ref, kpe_ref, ckvb_ref, kpeb_ref):
    kv = _dot_nt(x_ref[...].astype(BF16), w_ref[...].astype(BF16))
    ckv = _rms(kv[:, :KV_LORA], g_ref[...], RMS_EPS)
    kpe = _rope(kv[:, KV_LORA:], cos_ref[...], sin_ref[...])
    ckv_ref[...] = ckv
    kpe_ref[...] = kpe
    ckvb_ref[...] = ckv.astype(BF16)
    kpeb_ref[...] = kpe.astype(BF16)


def mla_kv(x, wdkv_t, g_kvn, cos2, sin2, *, tm=512):
    m, k = x.shape
    tm = min(tm, m)
    return pl.pallas_call(
        _mla_kv_body,
        grid=(m // tm,),
        in_specs=[pl.BlockSpec((tm, k), lambda i: (i, 0)), pl.BlockSpec((KV_LORA + ROPE_C, k), lambda i: (0, 0)),
                  pl.BlockSpec((1, KV_LORA), lambda i: (0, 0)),
                  pl.BlockSpec((tm, ROPE_C), lambda i: (i, 0)), pl.BlockSpec((tm, ROPE_C), lambda i: (i, 0))],
        out_specs=[pl.BlockSpec((tm, KV_LORA), lambda i: (i, 0)), pl.BlockSpec((tm, ROPE_C), lambda i: (i, 0)),
                   pl.BlockSpec((tm, KV_LORA), lambda i: (i, 0)), pl.BlockSpec((tm, ROPE_C), lambda i: (i, 0))],
        out_shape=[jax.ShapeDtypeStruct((m, KV_LORA), F32), jax.ShapeDtypeStruct((m, ROPE_C), F32),
                   jax.ShapeDtypeStruct((m, KV_LORA), BF16), jax.ShapeDtypeStruct((m, ROPE_C), BF16)],
        compiler_params=_cparams(("parallel",)),
        name="mla_kv",
    )(x, wdkv_t, g_kvn, cos2, sin2)


def _mla_kv_up_body(ckv_ref, kpe_ref, w_ref, k_ref, v_ref):
    up = _dot(ckv_ref[...], w_ref[...].astype(BF16))
    k_ref[:, :NOPE_C] = up[:, :NOPE_C].astype(k_ref.dtype)
    k_ref[:, NOPE_C:] = kpe_ref[...]
    v_ref[...] = up[:, NOPE_C:].astype(v_ref.dtype)


def mla_kv_up(ckv_b, kpe_b, w_ukv2, *, tm=2048):
    m = ckv_b.shape[0]
    tm = min(tm, m)
    return pl.pallas_call(
        _mla_kv_up_body,
        grid=(H_C, m // tm),
        in_specs=[pl.BlockSpec((tm, KV_LORA), lambda h, i: (i, 0)), pl.BlockSpec((tm, ROPE_C), lambda h, i: (i, 0)),
                  pl.BlockSpec((KV_LORA, NOPE_C + V_C), lambda h, i: (0, h))],
        out_specs=[pl.BlockSpec((None, tm, NOPE_C + ROPE_C), lambda h, i: (h, i, 0)),
                   pl.BlockSpec((None, tm, V_C), lambda h, i: (h, i, 0))],
        out_shape=[jax.ShapeDtypeStruct((H_C, m, NOPE_C + ROPE_C), ckv_b.dtype), jax.ShapeDtypeStruct((H_C, m, V_C), ckv_b.dtype)],
        compiler_params=_cparams(("parallel", "parallel")),
        name="mla_kv_up",
    )(ckv_b, kpe_b, w_ukv2)


def _mha_prompt_body(qi_ref, kj_ref, q_ref, k_ref, v_ref, o_ref, m_ref, acc_ref, *, t):
    n = pl.program_id(1)
    i = qi_ref[n]
    j = kj_ref[n]

    @pl.when(j == 0)
    def _():
        m_ref[...] = jnp.full_like(m_ref, NEG_INF)
        acc_ref[...] = jnp.zeros_like(acc_ref)

    qb = q_ref[...]
    kb = k_ref[...]
    v1 = _with_ones(v_ref[...])

    def update(masked):
        sub = t // MLA_SPLIT
        res = []
        for r in range(MLA_SPLIT):
            rows = slice(r * sub, (r + 1) * sub)
            sc = _dot_nt(qb[rows], kb)
            if masked:
                sc = jnp.where(_causal_tile(t)[rows], sc, NEG_INF)
            res.append((rows, _softmax_step(sc, v1, m_ref[rows], acc_ref[rows])))
        for rows, (m_new, acc_new) in res:
            m_ref[rows] = m_new
            acc_ref[rows] = acc_new

    @pl.when(j < i)
    def _():
        update(False)

    @pl.when(j == i)
    def _():
        update(True)
        acc = acc_ref[...]
        o_ref[...] = acc[:, :V_C] / acc[:, V_C:V_C + 1]


def mha_prompt(q3, k3, v3, *, t=1024):
    n_h, s, dq = q3.shape
    dv = v3.shape[2]
    t = min(t, s)
    qi, kj = _pair_tables(s // t)
    return pl.pallas_call(
        functools.partial(_mha_prompt_body, t=t),
        grid_spec=pltpu.PrefetchScalarGridSpec(
            num_scalar_prefetch=2,
            grid=(n_h, qi.shape[0]),
            in_specs=[pl.BlockSpec((None, t, dq), lambda hh, n, qi, kj: (hh, qi[n], 0)),
                      pl.BlockSpec((None, t, dq), lambda hh, n, qi, kj: (hh, kj[n], 0)),
                      pl.BlockSpec((None, t, dv), lambda hh, n, qi, kj: (hh, kj[n], 0))],
            out_specs=pl.BlockSpec((t, dv), lambda hh, n, qi, kj: (qi[n], hh)),
            scratch_shapes=[pltpu.VMEM((t, 1), F32), pltpu.VMEM((t, 2 * dv), F32)]),
        out_shape=jax.ShapeDtypeStruct((s, n_h * dv), F32),
        compiler_params=_cparams(("parallel", "arbitrary")),
        name="mha_prompt",
    )(qi, kj, q3, k3, v3)


def _mla_ov_body(o_ref, wuv_ref, out_ref):
    out_ref[...] = _dot(o_ref[...].astype(BF16), wuv_ref[...].astype(BF16))


def mla_ov(o3, w_ukv2, *, tm=512):
    m = o3.shape[1]
    tm = min(tm, m)
    return pl.pallas_call(
        _mla_ov_body,
        grid=(H_C, m // tm),
        in_specs=[pl.BlockSpec((None, tm, KV_LORA), lambda h, i: (h, i, 0)),
                  pl.BlockSpec((KV_LORA, V_C), lambda h, i: (0, 2 * h + 1))],
        out_specs=pl.BlockSpec((tm, V_C), lambda h, i: (i, h)),
        out_shape=jax.ShapeDtypeStruct((m, H_C * V_C), F32),
        compiler_params=_cparams(("parallel", "parallel")),
        name="mla_ov",
    )(o3, w_ukv2)


def _router_body(x_ref, w_ref, b_ref, gates_ref):
    x = x_ref[...]
    w = w_ref[...]
    xh = x.astype(BF16)
    xl = (x - xh.astype(F32)).astype(BF16)
    wh = w.astype(BF16)
    wl = (w - wh.astype(F32)).astype(BF16)
    logit = _dot(xh, wh) + _dot(xh, wl) + _dot(xl, wh) + b_ref[...]
    lane = lax.broadcasted_iota(jnp.int32, logit.shape, 1)
    big = jnp.int32(2 ** 30)

    def first_max(mask):
        v = jnp.max(jnp.where(mask, logit, NEG_INF), axis=-1, keepdims=True)
        idx = jnp.min(jnp.where(mask & (logit == v), lane, big), axis=-1, keepdims=True)
        return v, idx

    is_group = (lane >= N_EXPERTS) & (lane < N_EXPERTS + N_GROUPS)
    gmax, gidx = first_max(is_group)
    w_grp = 1.0 / jnp.sum(jnp.where(is_group, jnp.exp(logit - gmax), 0.0), axis=-1, keepdims=True)
    g_sel = gidx - N_EXPERTS
    in_grp = (lane >= g_sel * EXP_PER_GROUP) & (lane < (g_sel + 1) * EXP_PER_GROUP)
    v1, i1 = first_max(in_grp)
    v2, i2 = first_max(in_grp & (lane != i1))
    e2 = jnp.exp(v2 - v1)
    w1 = w_grp / (1.0 + e2)
    w2 = w_grp * e2 / (1.0 + e2)
    route = jnp.where(lane == 0, i1.astype(F32), 0.0) + jnp.where(lane == 1, i2.astype(F32), 0.0)
    gates_ref[...] = route + jnp.where(lane == 2, w1, 0.0) + jnp.where(lane == 3, w2, 0.0)


def router(x, w_pad, b_pad, *, tm=512):
    m, k = x.shape
    tm = min(tm, m)
    return pl.pallas_call(
        _router_body,
        grid=(m // tm,),
        in_specs=[pl.BlockSpec((tm, k), lambda i: (i, 0)), pl.BlockSpec((k, LANE), lambda i: (0, 0)),
                  pl.BlockSpec((1, LANE), lambda i: (0, 0))],
        out_specs=pl.BlockSpec((tm, LANE), lambda i: (i, 0)),
        out_shape=jax.ShapeDtypeStruct((m, LANE), F32),
        compiler_params=_cparams(("parallel",)),
        name="router",
    )(x, w_pad, b_pad)


MOE_TILE = 256
ROW_UNROLL = 8


def _row_gather(idx_smem, islot, src_hbm, dst, dslot, sem, n_rows, wait):
    def body(blk, carry):
        for k in range(ROW_UNROLL):
            r = blk * ROW_UNROLL + k
            row = idx_smem[islot, r]
            cp = pltpu.make_async_copy(src_hbm.at[pl.ds(row, 1), :], dst.at[dslot, pl.ds(r, 1), :], sem.at[dslot])
            cp.wait() if wait else cp.start(priority=k % 2)
        return carry
    lax.fori_loop(0, n_rows // ROW_UNROLL, body, 0)


def _gather_pipeline(t, n_act, idx_hbm, idx_smem, isem, src_hbm, dst, gsem, n_rows):
    idx_copy = lambda tile: pltpu.make_async_copy(idx_hbm.at[tile], idx_smem.at[tile % 3], isem.at[tile % 3])

    @pl.when(t == 0)
    def _():
        idx_copy(0).start()
        idx_copy(0).wait()
        _row_gather(idx_smem, 0, src_hbm, dst, 0, gsem, n_rows, wait=False)

        @pl.when(n_act > 1)
        def _():
            idx_copy(1).start()

    @pl.when(t + 1 < n_act)
    def _():
        idx_copy(t + 1).wait()
        _row_gather(idx_smem, (t + 1) % 3, src_hbm, dst, (t + 1) % 2, gsem, n_rows, wait=False)

        @pl.when(t + 2 < n_act)
        def _():
            idx_copy(t + 2).start()

    @pl.when(t < n_act)
    def _():
        _row_gather(idx_smem, t % 3, src_hbm, dst, t % 2, gsem, n_rows, wait=True)


def _moe_ffn_body(te_ref, na_ref, tok_hbm, x_hbm, gate_ref, wg_ref, wu_ref, wd_ref, o_ref,
                  xbuf, tok_smem, gsem, isem, wgb, wub, wdb):
    t = pl.program_id(0)
    n_act = na_ref[0]
    _gather_pipeline(t, n_act, tok_hbm, tok_smem, isem, x_hbm, xbuf, gsem, MOE_TILE)

    @pl.when(t < n_act)
    def _():
        @pl.when((t == 0) | (te_ref[t] != te_ref[jnp.maximum(t - 1, 0)]))
        def _():
            wgb[...] = wg_ref[...].astype(BF16)
            wub[...] = wu_ref[...].astype(BF16)
            wdb[...] = wd_ref[...].astype(BF16)

        xb = xbuf[t % 2].astype(BF16)
        a = _dot(xb, wgb[...])
        u = _dot(xb, wub[...])
        hmid = a * (1.0 / (1.0 + jnp.exp(-a))) * u * gate_ref[...]
        o_ref[...] = _dot(hmid.astype(BF16), wdb[...])

    @pl.when(t >= n_act)
    def _():
        o_ref[...] = jnp.zeros_like(o_ref)


def moe_ffn(tile_expert, n_active, tok_tiles, x, gate_col, w_gate, w_up, w_down, layer):
    n_tiles = tok_tiles.shape[0]
    d = x.shape[1]
    wspec = lambda shape: pl.BlockSpec((None, None) + shape, lambda t, te, na: (layer, te[t], 0, 0))
    return pl.pallas_call(
        _moe_ffn_body,
        grid_spec=pltpu.PrefetchScalarGridSpec(
            num_scalar_prefetch=2,
            grid=(n_tiles,),
            in_specs=[pl.BlockSpec(memory_space=pl.ANY), pl.BlockSpec(memory_space=pl.ANY),
                      pl.BlockSpec((MOE_TILE, 1), lambda t, te, na: (t, 0)),
                      wspec((d, D_EXPERT)), wspec((d, D_EXPERT)), wspec((D_EXPERT, d))],
            out_specs=pl.BlockSpec((MOE_TILE, d), lambda t, te, na: (t, 0)),
            scratch_shapes=[pltpu.VMEM((2, MOE_TILE, d), F32), pltpu.SMEM((3, MOE_TILE), jnp.int32),
                            pltpu.SemaphoreType.DMA((2,)), pltpu.SemaphoreType.DMA((3,)),
                            pltpu.VMEM((d, D_EXPERT), BF16), pltpu.VMEM((d, D_EXPERT), BF16),
                            pltpu.VMEM((D_EXPERT, d), BF16)]),
        out_shape=jax.ShapeDtypeStruct((n_tiles * MOE_TILE, d), F32),
        compiler_params=_cparams(("arbitrary",)),
        name="moe_ffn",
    )(tile_expert, n_active, tok_tiles, x, gate_col, w_gate, w_up, w_down)


def _moe_combine_body(pos_hbm, y_hbm, x_ref, g_ref, b_ref, o_ref, ybuf, pos_smem, gsem, isem, *, tm):
    t = pl.program_id(0)
    _gather_pipeline(t, pl.num_programs(0), pos_hbm, pos_smem, isem, y_hbm, ybuf, gsem, 2 * tm)
    y = ybuf[t % 2, :tm] + ybuf[t % 2, tm:]
    o_ref[...] = _layer_norm(ALPHA * x_ref[...] + y, g_ref[...], b_ref[...])


def moe_combine_ln(pos_tiles, y_sorted, x, g, b, *, tm):
    m, d = x.shape
    row = lambda i: (i, 0)
    fixed = lambda i: (0, 0)
    return pl.pallas_call(
        functools.partial(_moe_combine_body, tm=tm),
        grid=(m // tm,),
        in_specs=[pl.BlockSpec(memory_space=pl.ANY), pl.BlockSpec(memory_space=pl.ANY),
                  pl.BlockSpec((tm, d), row), pl.BlockSpec((1, d), fixed), pl.BlockSpec((1, d), fixed)],
        out_specs=pl.BlockSpec((tm, d), row),
        out_shape=jax.ShapeDtypeStruct((m, d), F32),
        scratch_shapes=[pltpu.VMEM((2, 2 * tm, d), F32), pltpu.SMEM((3, 2 * tm), jnp.int32),
                        pltpu.SemaphoreType.DMA((2,)), pltpu.SemaphoreType.DMA((3,))],
        compiler_params=_cparams(("arbitrary",)),
        name="moe_combine_ln",
    )(pos_tiles, y_sorted, x, g, b)


def _routing_tables(route, tm_tok):
    n = route.shape[0]
    n_asg = 2 * n
    n_tiles = (n_asg + N_EXPERTS * MOE_TILE) // MOE_TILE
    ef = route[:, 0:2].astype(jnp.int32).reshape(n_asg)
    wf = route[:, 2:4].reshape(n_asg)
    order = jnp.argsort(ef, stable=True).astype(jnp.int32)
    e_sorted = ef[order]
    bounds = jnp.searchsorted(e_sorted, jnp.arange(N_EXPERTS + 1, dtype=jnp.int32), side="left").astype(jnp.int32)
    counts = bounds[1:] - bounds[:-1]
    padded = ((counts + MOE_TILE - 1) // MOE_TILE) * MOE_TILE
    seg_end_pad = jnp.cumsum(padded)
    seg_start_pad = seg_end_pad - padded
    seg_start = jnp.cumsum(counts) - counts
    n_active = (seg_end_pad[-1] // MOE_TILE).astype(jnp.int32)
    tile_start = jnp.arange(n_tiles, dtype=jnp.int32) * MOE_TILE
    tile_expert = jnp.minimum(jnp.searchsorted(seg_end_pad, tile_start, side="right"), N_EXPERTS - 1).astype(jnp.int32)
    tile_expert = jnp.where(jnp.arange(n_tiles) < n_active, tile_expert, tile_expert[jnp.maximum(n_active - 1, 0)])
    p = jnp.arange(n_tiles * MOE_TILE, dtype=jnp.int32)
    e_p = jnp.repeat(tile_expert, MOE_TILE)
    local = p - seg_start_pad[e_p]
    valid = (local < counts[e_p]) & (p < seg_end_pad[-1])
    rank = jnp.clip(seg_start[e_p] + local, 0, n_asg - 1)
    asg = order[rank]
    row_token = jnp.where(valid, asg // 2, 0).astype(jnp.int32)
    row_gate = jnp.where(valid, wf[asg], 0.0)
    ppos = seg_start_pad[e_sorted] + (jnp.arange(n_asg, dtype=jnp.int32) - seg_start[e_sorted])
    pos_a = jnp.zeros((n_asg,), jnp.int32).at[order].set(ppos.astype(jnp.int32), unique_indices=True)
    pos_tiles = jnp.transpose(pos_a.reshape(n // tm_tok, tm_tok, 2), (0, 2, 1)).reshape(n // tm_tok, 2 * tm_tok)
    return (tile_expert, n_active.reshape(1), row_token.reshape(n_tiles, MOE_TILE),
            row_gate.reshape(n_tiles * MOE_TILE, 1), pos_tiles)


def _rope_tables(pos):
    half = ROPE_C // 2
    freqs = ROPE_THETA ** (-jnp.arange(half, dtype=F32) / half)
    ang = pos.astype(F32)[:, None] * freqs
    cos, sin = jnp.cos(ang), jnp.sin(ang)
    return jnp.concatenate([cos, cos], axis=1), jnp.concatenate([-sin, sin], axis=1)


def _lambda_init(layer_idx):
    return 0.8 - 0.6 * math.exp(-0.3 * layer_idx)


def _row(v):
    return v.reshape(1, -1)


def _moe_ln_block(x, l, g, b, w_router_group, b_router_group, w_router_expert, b_router_expert, w_gate, w_up, w_down):
    pad = jnp.zeros((D_MODEL, LANE - N_EXPERTS - N_GROUPS), F32)
    w_pad = jnp.concatenate([w_router_expert[l], w_router_group[l], pad], axis=1)
    b_pad = jnp.concatenate([b_router_expert[l], b_router_group[l], jnp.zeros((LANE - N_EXPERTS - N_GROUPS,), F32)])[None]
    tm_tok = math.gcd(x.shape[0], MOE_TILE)
    route = router(x, w_pad, b_pad, tm=tm_tok)
    tile_expert, n_active, tok_tiles, gate_col, pos_tiles = _routing_tables(route, tm_tok)
    y_sorted = moe_ffn(tile_expert, n_active, tok_tiles, x, gate_col, w_gate, w_up, w_down, l)
    return moe_combine_ln(pos_tiles, y_sorted, x, g, b, tm=tm_tok)


def kernel(x_prompt, x_sample, cache_diff_k, cache_diff_v, cache_fox_k, cache_fox_v, cache_fox_logf, cache_mla_ckv, cache_mla_kpe, page_table, w_in_even, b_forget, lambda_q1, lambda_k1, lambda_q2, lambda_k2, g_subln, w_out_even, w_dq, g_q_norm, w_uq, w_dkv, g_kv_norm, w_ukv, w_o_mla, ln_attn_g, ln_attn_b, ln_ffn_g, ln_ffn_b, w_router_group, b_router_group, w_router_expert, b_router_expert, w_gate, w_up, w_down):
    n_b, s_len, d = x_prompt.shape
    n_seq, t_new, _ = x_sample.shape
    assert n_b == 1 and t_new == T_S
    n_pool = cache_diff_k.shape[1]
    past = page_table.shape[1] * PAGE_SIZE
    xp = x_prompt.reshape(s_len, d)
    xs = x_sample.reshape(n_seq * T_S, d)
    slopes = 2.0 ** (-8.0 * jnp.arange(1, H_A + 1, dtype=F32) / H_A)
    col = np.arange(OFF_FL)
    col_scale = jnp.asarray(np.where(col < OFF_KD, DIFF_SCALE, np.where((col >= OFF_QF) & (col < OFF_KF), FOX_SCALE, 1.0)),
                            F32).reshape(1, OFF_FL)
    moe_w = (w_router_group, b_router_group, w_router_expert, b_router_expert, w_gate, w_up, w_down)
    outs_p, outs_s = {}, {}

    for l in range(DEPTH):
        i = l // 2
        if l % 2 == 0:
            lam_init = _lambda_init(l)
            w_t = jnp.transpose(w_in_even[i])
            wft = w_t[OFF_FL:OFF_FL + H_B]
            wf_pad = jnp.concatenate([jnp.transpose(wft), jnp.zeros((d, LANE - H_B), F32)], axis=1)
            b_row = jnp.concatenate([b_forget[i], jnp.zeros((LANE - H_B,), F32)])[None]
            b_col = b_forget[i][:, None]
            lam_args = (_row(lambda_q1[i]), _row(lambda_k1[i]), _row(lambda_q2[i]), _row(lambda_k2[i]), _row(g_subln[i]))
            hp, hp_b = matmul_scaled(xp, w_t, col_scale, nt=True, n_out=OFF_FL, name="even_proj_p")
            lf_p, cum_p, cumt_p = forget_gates(xp, wf_pad, wft, b_row, b_col, seg=s_len)
            od = diff_prompt(hp_b, slopes, *lam_args, lam_init)
            of = fox_prompt(hp_b, cum_p, cumt_p.reshape(H_B, 1, s_len))
            xp = matmul_ln(jnp.concatenate([od, of], axis=1), w_out_even[i], xp,
                           _row(ln_attn_g[l]), _row(ln_attn_b[l]), name="even_out_p")
            outs_p[l] = (hp[:, OFF_KD:OFF_VD].reshape(1, s_len, KV_A, 2, DQK_A), hp[:, OFF_VD:OFF_QF].reshape(1, s_len, KV_A, DV_A),
                         hp[:, OFF_KF:OFF_VF].reshape(1, s_len, KV_B, D_B), hp[:, OFF_VF:OFF_FL].reshape(1, s_len, KV_B, D_B),
                         lf_p[:, :H_B].reshape(1, s_len, H_B))
            hs, _ = matmul_scaled(xs, w_t, col_scale, nt=True, n_out=OFF_FL, name="even_proj_s")
            lf_s, cum_s, cumt_s = forget_gates(xs, wf_pad, wft, b_row, b_col, seg=T_S)
            cst = jnp.transpose(cumt_s.reshape(H_B, n_seq, T_S), (1, 0, 2))
            cst = jnp.concatenate([cst, jnp.zeros((n_seq, H_B, LANE - T_S), F32)], axis=2)
            dk_t = jnp.transpose(cache_diff_k[i], (0, 2, 3, 4, 1)).reshape(n_pool, KV_A * 2 * DQK_A, PAGE_SIZE)
            dv2 = cache_diff_v[i].reshape(n_pool, PAGE_SIZE * KV_A, DV_A)
            fk2 = cache_fox_k[i].reshape(n_pool, PAGE_SIZE * KV_B, D_B)
            fv2 = cache_fox_v[i].reshape(n_pool, PAGE_SIZE * KV_B, D_B)
            lf_t = jnp.transpose(cache_fox_logf[i], (0, 2, 1))
            od, of = even_sample(page_table, hs, cum_s, cst, *lam_args, dk_t, dv2, fk2, fv2, lf_t, lam_init)
            xs = matmul_ln(jnp.concatenate([od, of], axis=1), w_out_even[i], xs,
                           _row(ln_attn_g[l]), _row(ln_attn_b[l]), name="even_out_s")
            outs_s[l] = (hs[:, OFF_KD:OFF_VD].reshape(n_seq, T_S, KV_A, 2, DQK_A), hs[:, OFF_VD:OFF_QF].reshape(n_seq, T_S, KV_A, DV_A),
                         hs[:, OFF_KF:OFF_VF].reshape(n_seq, T_S, KV_B, D_B), hs[:, OFF_VF:OFF_FL].reshape(n_seq, T_S, KV_B, D_B),
                         lf_s[:, :H_B].reshape(n_seq, T_S, H_B))
        else:
            wuq_t = jnp.transpose(w_uq[i], (1, 2, 0))
            wdkv_t = jnp.transpose(w_dkv[i])
            w_ukv2 = w_ukv[i].reshape(KV_LORA, H_C * (NOPE_C + V_C))
            kpe_t = jnp.transpose(cache_mla_kpe[i], (0, 2, 1))
            for grp in ("p", "s"):
                x = xp if grp == "p" else xs
                pos = jnp.arange(s_len, dtype=jnp.int32) if grp == "p" else past + jnp.tile(jnp.arange(T_S, dtype=jnp.int32), n_seq)
                cos2, sin2 = _rope_tables(pos)
                cq = mla_cq(x, w_dq[i], _row(g_q_norm[i]))
                ckv, kpe, ckv_b, kpe_b = mla_kv(x, wdkv_t, _row(g_kv_norm[i]), cos2, sin2)
                if grp == "p":
                    q3 = mla_q(cq, wuq_t, w_ukv2, cos2, sin2, scale=MLA_SCALE, out_dtype=BF16, absorb=False)
                    k3, v3 = mla_kv_up(ckv_b, kpe_b, w_ukv2)
                    ov = mha_prompt(q3, k3, v3)
                else:
                    q3 = mla_q(cq, wuq_t, w_ukv2, cos2, sin2)
                    o3 = mla_sample(page_table, q3, ckv, kpe, cache_mla_ckv[i], kpe_t)
                    ov = mla_ov(o3, w_ukv2)
                x = matmul_ln(ov, w_o_mla[i], x, _row(ln_attn_g[l]), _row(ln_attn_b[l]), name="mla_out_" + grp)
                if grp == "p":
                    xp = x
                    outs_p[l] = (ckv.reshape(1, s_len, KV_LORA), kpe.reshape(1, s_len, ROPE_C))
                else:
                    xs = x
                    outs_s[l] = (ckv.reshape(n_seq, T_S, KV_LORA), kpe.reshape(n_seq, T_S, ROPE_C))
        x_all = _moe_ln_block(jnp.concatenate([xp, xs], axis=0), l, _row(ln_ffn_g[l]), _row(ln_ffn_b[l]), *moe_w)
        xp, xs = x_all[:s_len], x_all[s_len:]

    stack = lambda outs, k, ls: jnp.stack([outs[l][k] for l in ls])
    even, odd = range(0, DEPTH, 2), range(1, DEPTH, 2)
    return (xp.reshape(1, s_len, d), xs.reshape(n_seq, T_S, d),
            *[stack(outs_p, k, even) for k in range(5)], *[stack(outs_p, k, odd) for k in range(2)],
            *[stack(outs_s, k, even) for k in range(5)], *[stack(outs_s, k, odd) for k in range(2)])
```
